```python
import math
import jax
import jax.numpy as jnp
from jax import lax
import numpy as np

D_MODEL = 4096
BATCH = 4
SEQ = 4096
DEPTH = 2
DEC_BATCH = 8
DEC_SEQ = 32
PAST_LEN = 4096

CHUNK = 64
Q_BLOCK = 128
DN_ALPHA = (2 * DEPTH) ** 0.25
DN_BETA = (8 * DEPTH) ** -0.25
LN_EPS = 1e-5
RMS_EPS = 1e-6

S5_WIDTH = D_MODEL // 2
S5_GROUP = 16
S5_GROUPS = S5_WIDTH // S5_GROUP
S5_STATE = 64
S5_DT_MIN = 1e-3
S5_DT_MAX = 1e-1

RWKV_WIDTH = D_MODEL - S5_WIDTH
RWKV_HEAD = 64
RWKV_HEADS = RWKV_WIDTH // RWKV_HEAD
RWKV_DECAY_LORA = max(32, int(round(1.8 * RWKV_WIDTH ** 0.5 / 32)) * 32)
RWKV_AAA_LORA = max(32, int(round(1.8 * RWKV_WIDTH ** 0.5 / 32)) * 32)
RWKV_GATE_LORA = max(32, int(round(0.6 * RWKV_WIDTH ** 0.8 / 32)) * 32)
RWKV_PROJ = 3 * RWKV_WIDTH + RWKV_DECAY_LORA + RWKV_AAA_LORA + RWKV_GATE_LORA
RWKV_GN_EPS = 64e-5
L0_IN = S5_WIDTH + RWKV_PROJ

MLA_HEADS = 64
MLA_Q_LORA = 1024
MLA_KV_LORA = 512
MLA_NOPE = 128
MLA_ROPE = 64
MLA_V = 128
ROPE_THETA = 10000.0
L1_IN = MLA_Q_LORA + MLA_KV_LORA + MLA_ROPE

N_EXPERTS = 128
TOP_K = 8
N_GROUP = 8
TOPK_GROUP = 4
EXPERT_FF = 1024
SHARED_FF = 1024
ROUTED_SCALE = 2.5

kernel_name = 'streaming_s5_rwkv7_mla_moe_step'


def _layer_norm(x, g, b):
    xf = x.astype(jnp.float32)
    mu = xf.mean(-1, keepdims=True)
    var = jnp.mean(jnp.square(xf - mu), -1, keepdims=True)
    y = (xf - mu) * lax.rsqrt(var + LN_EPS) * g.astype(jnp.float32) + b.astype(jnp.float32)
    return y.astype(x.dtype)


def _rms_norm(x, g):
    xf = x.astype(jnp.float32)
    y = xf * lax.rsqrt(jnp.mean(jnp.square(xf), -1, keepdims=True) + RMS_EPS) * g.astype(jnp.float32)
    return y.astype(x.dtype)


def _rope(x, pos):
    half = MLA_ROPE // 2
    inv = jnp.exp(-math.log(ROPE_THETA) * jnp.arange(half, dtype=jnp.float32) / half)
    ang = pos.astype(jnp.float32)[:, None] * inv[None, :]
    shape = (1, pos.shape[0]) + (1,) * (x.ndim - 3) + (half,)
    cos = jnp.cos(ang).reshape(shape)
    sin = jnp.sin(ang).reshape(shape)
    x1 = x[..., :half].astype(jnp.float32)
    x2 = x[..., half:].astype(jnp.float32)
    return jnp.concatenate([x1 * cos - x2 * sin, x1 * sin + x2 * cos], -1).astype(x.dtype)


def _chunk_causal_attention(q, k, v, q_pos, k_pos, scale):
    bsz, lq, g, r, dk = q.shape
    qb = Q_BLOCK if lq % Q_BLOCK == 0 else lq
    nb = lq // qb
    kf = k.astype(jnp.float32)
    vf = v.astype(jnp.float32)
    k_chunk = k_pos // CHUNK
    q_blocks = jnp.moveaxis(q.astype(jnp.float32).reshape(bsz, nb, qb, g, r, dk), 1, 0)
    pos_blocks = q_pos.reshape(nb, qb)

    def attend(args):
        qblk, pblk = args
        s = jnp.einsum('bqgrd,bkgd->bgrqk', qblk, kf) * scale
        visible = k_chunk[None, :] <= (pblk // CHUNK)[:, None]
        p = jax.nn.softmax(jnp.where(visible, s, -jnp.inf), axis=-1)
        return jnp.einsum('bgrqk,bkgd->bqgrd', p, vf)

    o = lax.map(attend, (q_blocks, pos_blocks))
    return jnp.moveaxis(o, 0, 1).reshape(bsz, lq, g, r, vf.shape[-1]).astype(v.dtype)


def _s5_mixer(u, h0_re, h0_im, lam_re, lam_im, b_re, b_im, c_re, c_im, d, log_dt, w_glu, b_glu):
    f32 = jnp.float32
    bsz, seq, _ = u.shape
    lam = lax.complex(lam_re.astype(f32), lam_im.astype(f32))
    dt = jnp.exp(log_dt.astype(f32))[:, None]
    lam_bar = jnp.exp(lam * dt)
    b_bar = ((lam_bar - 1.0) / lam)[..., None] * lax.complex(b_re.astype(f32), b_im.astype(f32))
    ug = u.astype(f32).reshape(bsz, seq, S5_GROUPS, S5_GROUP)
    bu = lax.complex(jnp.einsum('blgh,gph->blgp', ug, b_bar.real),
                     jnp.einsum('blgh,gph->blgp', ug, b_bar.imag))
    h0 = lax.complex(h0_re.astype(f32), h0_im.astype(f32))
    bu = bu.at[:, 0].add(lam_bar * h0)
    a = jnp.broadcast_to(lam_bar, bu.shape)

    def combine(e1, e2):
        a1, b1 = e1
        a2, b2 = e2
        return a1 * a2, a2 * b1 + b2

    _, h = lax.associative_scan(combine, (a, bu), axis=1)
    y = (jnp.einsum('blgp,ghp->blgh', h.real, c_re.astype(f32))
         - jnp.einsum('blgp,ghp->blgh', h.imag, c_im.astype(f32)))
    y = y.reshape(bsz, seq, S5_WIDTH) + d.astype(f32) * u.astype(f32)
    z = jax.nn.gelu(y).astype(u.dtype)
    out = z * jax.nn.sigmoid(z @ w_glu + b_glu)
    h_last = h[:, -1]
    return out, h_last.real.astype(h0_re.dtype), h_last.imag.astype(h0_im.dtype)


def _rwkv7_mixer(p, shift0, s0, mu, w0, w2, a0, a2, g2, k_k, k_a, r_k, ln_w, ln_b):
    f32 = jnp.float32
    bsz, seq, _ = p.shape
    c = RWKV_WIDTH
    p_prev = jnp.concatenate([shift0[:, None].astype(p.dtype), p[:, :-1]], axis=1)
    xm = p + (p_prev - p) * mu
    r, k, v, w_lo, a_lo, g_lo = jnp.split(
        xm, [c, 2 * c, 3 * c, 3 * c + RWKV_DECAY_LORA, 3 * c + RWKV_DECAY_LORA + RWKV_AAA_LORA], axis=-1)
    w_log = -jax.nn.softplus(-(w0 + jnp.tanh(w_lo) @ w2).astype(f32)) - 0.5
    decay = jnp.exp(-jnp.exp(w_log))
    a = jax.nn.sigmoid((a0 + a_lo @ a2).astype(f32))
    g = (jax.nn.sigmoid(g_lo) @ g2).astype(f32)
    heads = lambda t: t.astype(f32).reshape(bsz, seq, RWKV_HEADS, RWKV_HEAD)
    kk = heads(k * k_k)
    kk = kk * lax.rsqrt(jnp.maximum(jnp.sum(jnp.square(kk), -1, keepdims=True), 1e-24))
    k_mod = k.astype(f32) * (1.0 + (a - 1.0) * k_a.astype(f32))
    r_h, k_h, v_h, w_h, a_h = heads(r), heads(k_mod), heads(v), heads(decay), heads(a)

    def step(s, inp):
        r_t, w_t, k_t, v_t, kk_t, a_t = inp
        sk = jnp.einsum('bhvk,bhk->bhv', s, kk_t)
        s = (s * w_t[:, :, None, :] - sk[..., None] * (kk_t * a_t)[:, :, None, :]
             + v_t[..., None] * k_t[:, :, None, :])
        return s, jnp.einsum('bhvk,bhk->bhv', s, r_t)

    xs = tuple(jnp.moveaxis(t, 1, 0) for t in (r_h, w_h, k_h, v_h, kk, a_h))
    s_last, ys = lax.scan(step, s0.astype(f32), xs)
    y = jnp.moveaxis(ys, 0, 1)
    ym = y.mean(-1, keepdims=True)
    yv = jnp.mean(jnp.square(y - ym), -1, keepdims=True)
    y = ((y - ym) * lax.rsqrt(yv + RWKV_GN_EPS)).reshape(bsz, seq, c) * ln_w.astype(f32) + ln_b.astype(f32)
    bonus = (jnp.sum(r_h * k_h * r_k.astype(f32), -1, keepdims=True) * v_h).reshape(bsz, seq, c)
    out = ((y + bonus) * g).astype(p.dtype)
    return out, s_last.astype(s0.dtype), p[:, -1].astype(shift0.dtype)


def _mla_mixer(x, pos, past_ckv, past_krope, w_in, q_norm, w_uq, kv_norm, w_uk, w_uv, w_out):
    bsz, seq, _ = x.shape
    proj = x @ w_in
    cq, ckv, kr = jnp.split(proj, [MLA_Q_LORA, MLA_Q_LORA + MLA_KV_LORA], axis=-1)
    q = jnp.einsum('blc,chd->blhd', _rms_norm(cq, q_norm), w_uq)
    q_nope = q[..., :MLA_NOPE]
    q_rope = _rope(q[..., MLA_NOPE:], pos)
    ckv = _rms_norm(ckv, kv_norm)
    kr = _rope(kr, pos)
    scale = (MLA_NOPE + MLA_ROPE) ** -0.5
    if past_ckv is None:
        k_nope = jnp.einsum('blc,chd->blhd', ckv, w_uk)
        v = jnp.einsum('blc,chd->blhd', ckv, w_uv)
        k = jnp.concatenate([k_nope, jnp.broadcast_to(kr[:, :, None, :], (bsz, seq, MLA_HEADS, MLA_ROPE))], -1)
        qf = jnp.concatenate([q_nope, q_rope], -1)[:, :, :, None, :]
        o = _chunk_causal_attention(qf, k, v, pos, pos, scale)[:, :, :, 0]
    else:
        past = past_ckv.shape[1]
        ckv_all = jnp.concatenate([past_ckv.astype(ckv.dtype), ckv], axis=1)
        kr_all = jnp.concatenate([past_krope.astype(kr.dtype), kr], axis=1)
        k_pos = jnp.concatenate([jnp.arange(past, dtype=pos.dtype), pos])
        q_lat = jnp.einsum('blhd,chd->blhc', q_nope, w_uk)
        q_abs = jnp.concatenate([q_lat, q_rope], -1)[:, :, None]
        k_abs = jnp.concatenate([ckv_all, kr_all], -1)[:, :, None]
        o_lat = _chunk_causal_attention(q_abs, k_abs, ckv_all[:, :, None], pos, k_pos, scale)[:, :, 0]
        o = jnp.einsum('blhc,chd->blhd', o_lat, w_uv)
    out = o.reshape(bsz, seq, MLA_HEADS * MLA_V) @ w_out
    return out, ckv, kr


def _expert_rows(n_asg):
    return int(min(128, max(8, n_asg // (4 * N_EXPERTS))))


def _moe_ffn(h, layer, w_router, b_router, w_gate, w_up, w_down, ws_gate, ws_up, ws_down):
    f32 = jnp.float32
    n_tok, d = h.shape
    scores = jax.nn.sigmoid(jnp.dot(h.astype(f32), w_router[layer].astype(f32)))
    biased = scores + b_router[layer].astype(f32)
    grp_score = lax.top_k(biased.reshape(n_tok, N_GROUP, N_EXPERTS // N_GROUP), 2)[0].sum(-1)
    _, grp_idx = lax.top_k(grp_score, TOPK_GROUP)
    grp_keep = jnp.any(grp_idx[:, :, None] == jnp.arange(N_GROUP)[None, None, :], axis=1)
    keep = jnp.repeat(grp_keep, N_EXPERTS // N_GROUP, axis=-1)
    _, top_idx = lax.top_k(jnp.where(keep, biased, -jnp.inf), TOP_K)
    top_w = jnp.take_along_axis(scores, top_idx, axis=-1)
    top_w = top_w / (top_w.sum(-1, keepdims=True) + 1e-20) * ROUTED_SCALE
    n_asg = n_tok * TOP_K
    rows = _expert_rows(n_asg)
    n_blk = -(-n_asg // rows) + N_EXPERTS
    flat_e = top_idx.reshape(-1)
    order = jnp.argsort(flat_e)
    e_sorted = flat_e[order]
    counts = jnp.bincount(flat_e, length=N_EXPERTS)
    padded = (counts + rows - 1) // rows * rows
    pad_end = jnp.cumsum(padded)
    pad_start = pad_end - padded
    start = jnp.cumsum(counts) - counts
    dest = pad_start[e_sorted] + jnp.arange(n_asg) - start[e_sorted]
    slot_tok = jnp.full((n_blk * rows,), n_tok, jnp.int32).at[dest].set((order // TOP_K).astype(jnp.int32))
    slot_w = jnp.zeros((n_blk * rows,), h.dtype).at[dest].set(top_w.reshape(-1)[order].astype(h.dtype))
    blk_expert = jnp.minimum(jnp.searchsorted(pad_end, jnp.arange(n_blk) * rows, side='right'), N_EXPERTS - 1)
    h_pad = jnp.concatenate([h, jnp.zeros((1, d), h.dtype)], axis=0)

    def expert_block(args):
        toks, wts, e = args
        xb = h_pad[toks]
        hid = jax.nn.silu(xb @ w_gate[layer, e]) * (xb @ w_up[layer, e])
        return (hid @ w_down[layer, e]) * wts[:, None]

    y_slots = lax.map(expert_block, (slot_tok.reshape(n_blk, rows), slot_w.reshape(n_blk, rows), blk_expert))
    routed = jax.ops.segment_sum(y_slots.reshape(n_blk * rows, d), slot_tok, num_segments=n_tok + 1)[:n_tok]
    shared = (jax.nn.silu(h @ ws_gate[layer]) * (h @ ws_up[layer])) @ ws_down[layer]
    return routed + shared


def _trunk(x, pos, s5_re0, s5_im0, rwkv_s0, rwkv_shift0, past_ckv, past_krope,
           l0_w_in, l0_w_out, s5_p, rwkv_p, l1_w_in, l1_w_out, mla_p, norm_p, moe_p):
    ln_mix_g, ln_mix_b, ln_ffn_g, ln_ffn_b = norm_p
    bsz, seq, _ = x.shape
    for layer in range(DEPTH):
        if layer % 2 == 0:
            proj = x @ l0_w_in
            y_a, s5_re, s5_im = _s5_mixer(proj[..., :S5_WIDTH], s5_re0, s5_im0, *s5_p)
            y_b, rwkv_s, rwkv_shift = _rwkv7_mixer(proj[..., S5_WIDTH:], rwkv_shift0, rwkv_s0, *rwkv_p)
            mix = jnp.concatenate([y_a, y_b], axis=-1) @ l0_w_out
        else:
            mix, ckv_new, kr_new = _mla_mixer(x, pos, past_ckv, past_krope, l1_w_in, *mla_p, l1_w_out)
        x = _layer_norm(DN_ALPHA * x + mix, ln_mix_g[layer], ln_mix_b[layer])
        ffn = _moe_ffn(x.reshape(bsz * seq, D_MODEL), layer, *moe_p).reshape(x.shape)
        x = _layer_norm(DN_ALPHA * x + ffn, ln_ffn_g[layer], ln_ffn_b[layer])
    return x, s5_re, s5_im, rwkv_s, rwkv_shift, ckv_new, kr_new


def setup_inputs(seed: int = 0) -> dict:
    key = jax.random.key(seed)
    ks = iter(jax.random.split(key, 64))
    f32 = jnp.float32
    nrm = lambda shape, scale: jax.random.normal(next(ks), shape, f32) * scale
    d = D_MODEL
    inputs = {}
    inputs['x_prompt'] = nrm((BATCH, SEQ, d), 1.0)
    inputs['x_sample'] = nrm((DEC_BATCH, DEC_SEQ, d), 1.0)
    inputs['state_s5_re'] = nrm((DEC_BATCH, S5_GROUPS, S5_STATE), 0.1)
    inputs['state_s5_im'] = nrm((DEC_BATCH, S5_GROUPS, S5_STATE), 0.1)
    inputs['state_rwkv'] = nrm((DEC_BATCH, RWKV_HEADS, RWKV_HEAD, RWKV_HEAD), 0.1)
    inputs['state_rwkv_shift'] = nrm((DEC_BATCH, RWKV_PROJ), 1.0)
    inputs['cache_c_kv'] = nrm((DEC_BATCH, PAST_LEN, MLA_KV_LORA), 1.0)
    inputs['cache_k_rope'] = nrm((DEC_BATCH, PAST_LEN, MLA_ROPE), 1.0)
    inputs['l0_w_in'] = nrm((d, L0_IN), d ** -0.5)
    inputs['s5_lambda_re'] = -0.5 + nrm((S5_GROUPS, S5_STATE), 0.01)
    inputs['s5_lambda_im'] = jnp.pi * jnp.arange(S5_STATE, dtype=f32)[None, :] + nrm((S5_GROUPS, S5_STATE), 0.01)
    inputs['s5_b_re'] = nrm((S5_GROUPS, S5_STATE, S5_GROUP), (2 * S5_GROUP) ** -0.5)
    inputs['s5_b_im'] = nrm((S5_GROUPS, S5_STATE, S5_GROUP), (2 * S5_GROUP) ** -0.5)
    inputs['s5_c_re'] = nrm((S5_GROUPS, S5_GROUP, S5_STATE), 0.5)
    inputs['s5_c_im'] = nrm((S5_GROUPS, S5_GROUP, S5_STATE), 0.5)
    inputs['s5_d'] = nrm((S5_WIDTH,), 1.0)
    inputs['s5_log_dt'] = jax.random.uniform(next(ks), (S5_GROUPS,), f32, math.log(S5_DT_MIN), math.log(S5_DT_MAX))
    inputs['s5_w_glu'] = nrm((S5_WIDTH, S5_WIDTH), S5_WIDTH ** -0.5)
    inputs['s5_b_glu'] = nrm((S5_WIDTH,), 0.01)
    inputs['rwkv_mu'] = jax.random.uniform(next(ks), (RWKV_PROJ,), f32)
    inputs['rwkv_w0'] = jnp.linspace(-6.5, -1.5, RWKV_WIDTH, dtype=f32) + nrm((RWKV_WIDTH,), 0.01)
    inputs['rwkv_w2'] = nrm((RWKV_DECAY_LORA, RWKV_WIDTH), 0.1 * RWKV_DECAY_LORA ** -0.5)
    inputs['rwkv_a0'] = nrm((RWKV_WIDTH,), 0.01)
    inputs['rwkv_a2'] = nrm((RWKV_AAA_LORA, RWKV_WIDTH), 0.1 * RWKV_AAA_LORA ** -0.5)
    inputs['rwkv_g2'] = nrm((RWKV_GATE_LORA, RWKV_WIDTH), RWKV_GATE_LORA ** -0.5)
    inputs['rwkv_k_k'] = 0.85 + nrm((RWKV_WIDTH,), 0.01)
    inputs['rwkv_k_a'] = 1.0 + nrm((RWKV_WIDTH,), 0.01)
    inputs['rwkv_r_k'] = -0.04 + nrm((RWKV_HEADS, RWKV_HEAD), 0.02)
    inputs['rwkv_ln_w'] = 1.0 + nrm((RWKV_WIDTH,), 0.02)
    inputs['rwkv_ln_b'] = nrm((RWKV_WIDTH,), 0.02)
    inputs['l0_w_out'] = nrm((S5_WIDTH + RWKV_WIDTH, d), DN_BETA * (S5_WIDTH + RWKV_WIDTH) ** -0.5)
    inputs['l1_w_in'] = nrm((d, L1_IN), d ** -0.5)
    inputs['mla_q_norm'] = 1.0 + nrm((MLA_Q_LORA,), 0.02)
    inputs['mla_w_uq'] = nrm((MLA_Q_LORA, MLA_HEADS, MLA_NOPE + MLA_ROPE), MLA_Q_LORA ** -0.5)
    inputs['mla_kv_norm'] = 1.0 + nrm((MLA_KV_LORA,), 0.02)
    inputs['mla_w_uk'] = nrm((MLA_KV_LORA, MLA_HEADS, MLA_NOPE), MLA_KV_LORA ** -0.5)
    inputs['mla_w_uv'] = nrm((MLA_KV_LORA, MLA_HEADS, MLA_V), MLA_KV_LORA ** -0.5)
    inputs['l1_w_out'] = nrm((MLA_HEADS * MLA_V, d), DN_BETA * (MLA_HEADS * MLA_V) ** -0.5)
    inputs['ln_mix_g'] = 1.0 + nrm((DEPTH, d), 0.02)
    inputs['ln_mix_b'] = nrm((DEPTH, d), 0.02)
    inputs['ln_ffn_g'] = 1.0 + nrm((DEPTH, d), 0.02)
    inputs['ln_ffn_b'] = nrm((DEPTH, d), 0.02)
    inputs['moe_w_router'] = nrm((DEPTH, d, N_EXPERTS), d ** -0.5)
    inputs['moe_b_router'] = nrm((DEPTH, N_EXPERTS), 0.01)
    inputs['moe_w_gate'] = nrm((DEPTH, N_EXPERTS, d, EXPERT_FF), d ** -0.5)
    inputs['moe_w_up'] = nrm((DEPTH, N_EXPERTS, d, EXPERT_FF), d ** -0.5)
    inputs['moe_w_down'] = nrm((DEPTH, N_EXPERTS, EXPERT_FF, d), DN_BETA * EXPERT_FF ** -0.5)
    inputs['moe_ws_gate'] = nrm((DEPTH, d, SHARED_FF), d ** -0.5)
    inputs['moe_ws_up'] = nrm((DEPTH, d, SHARED_FF), d ** -0.5)
    inputs['moe_ws_down'] = nrm((DEPTH, SHARED_FF, d), DN_BETA * SHARED_FF ** -0.5)
    return inputs


def reference(x_prompt, x_sample, state_s5_re, state_s5_im, state_rwkv, state_rwkv_shift, cache_c_kv, cache_k_rope,
              l0_w_in, s5_lambda_re, s5_lambda_im, s5_b_re, s5_b_im, s5_c_re, s5_c_im, s5_d, s5_log_dt,
              s5_w_glu, s5_b_glu, rwkv_mu, rwkv_w0, rwkv_w2, rwkv_a0, rwkv_a2, rwkv_g2, rwkv_k_k, rwkv_k_a,
              rwkv_r_k, rwkv_ln_w, rwkv_ln_b, l0_w_out, l1_w_in, mla_q_norm, mla_w_uq, mla_kv_norm, mla_w_uk,
              mla_w_uv, l1_w_out, ln_mix_g, ln_mix_b, ln_ffn_g, ln_ffn_b, moe_w_router, moe_b_router,
              moe_w_gate, moe_w_up, moe_w_down, moe_ws_gate, moe_ws_up, moe_ws_down):
    s5_p = (s5_lambda_re, s5_lambda_im, s5_b_re, s5_b_im, s5_c_re, s5_c_im, s5_d, s5_log_dt, s5_w_glu, s5_b_glu)
    rwkv_p = (rwkv_mu, rwkv_w0, rwkv_w2, rwkv_a0, rwkv_a2, rwkv_g2, rwkv_k_k, rwkv_k_a, rwkv_r_k, rwkv_ln_w, rwkv_ln_b)
    mla_p = (mla_q_norm, mla_w_uq, mla_kv_norm, mla_w_uk, mla_w_uv)
    norm_p = (ln_mix_g, ln_mix_b, ln_ffn_g, ln_ffn_b)
    moe_p = (moe_w_router, moe_b_router, moe_w_gate, moe_w_up, moe_w_down, moe_ws_gate, moe_ws_up, moe_ws_down)

    bp, lp = x_prompt.shape[0], x_prompt.shape[1]
    sdt = state_s5_re.dtype
    (y_prompt, s5_re_prompt, s5_im_prompt, rwkv_state_prompt, rwkv_shift_prompt,
     c_kv_prompt, k_rope_prompt) = _trunk(
        x_prompt, jnp.arange(lp, dtype=jnp.int32),
        jnp.zeros((bp, S5_GROUPS, S5_STATE), sdt), jnp.zeros((bp, S5_GROUPS, S5_STATE), sdt),
        jnp.zeros((bp, RWKV_HEADS, RWKV_HEAD, RWKV_HEAD), state_rwkv.dtype),
        jnp.zeros((bp, RWKV_PROJ), state_rwkv_shift.dtype), None, None,
        l0_w_in, l0_w_out, s5_p, rwkv_p, l1_w_in, l1_w_out, mla_p, norm_p, moe_p)

    past = cache_c_kv.shape[1]
    ls = x_sample.shape[1]
    (y_sample, s5_re_sample, s5_im_sample, rwkv_state_sample, rwkv_shift_sample,
     c_kv_sample, k_rope_sample) = _trunk(
        x_sample, past + jnp.arange(ls, dtype=jnp.int32),
        state_s5_re, state_s5_im, state_rwkv, state_rwkv_shift, cache_c_kv, cache_k_rope,
        l0_w_in, l0_w_out, s5_p, rwkv_p, l1_w_in, l1_w_out, mla_p, norm_p, moe_p)

    return (y_prompt, y_sample,
            s5_re_prompt, s5_im_prompt, rwkv_state_prompt, rwkv_shift_prompt, c_kv_prompt, k_rope_prompt,
            s5_re_sample, s5_im_sample, rwkv_state_sample, rwkv_shift_sample, c_kv_sample, k_rope_sample)
```

```python
import functools
import math

import jax
import jax.numpy as jnp
from jax import lax
from jax.experimental import pallas as pl
from jax.experimental.pallas import tpu as pltpu

F32 = jnp.float32
BF16 = jnp.bfloat16

CHUNK = 64
LN_EPS = 1e-5
RMS_EPS = 1e-6
RWKV_GN_EPS = 64e-5
ROPE_THETA = 10000.0
TOP_K = 8
N_GROUP = 8
TOPK_GROUP = 4
ROUTED_SCALE = 2.5
MASK_VALUE = -1e30

LANES = 128
V7X_VMEM_BYTES = 64 * 2**20
VMEM_LIMIT = V7X_VMEM_BYTES - 8 * 2**20

MM_TM = 1024
MM_TN = 1024
MM_TK = 4096
LN_TM = 256
S5_T = 512
RWKV_PREP_T = 256
RWKV_SCAN_T = 64
ATTN_TQ = 512
MOE_TM = 256
MOE_TF = 512
MOE_TN = 2048
ROUTER_TM = 512


def _tile(n, target, align):
    if n <= target:
        return n
    t = (target // align) * align
    while t >= align:
        if n % t == 0:
            return t
        t -= align
    return n


def _cparams(*sems):
    return pltpu.CompilerParams(dimension_semantics=sems, vmem_limit_bytes=VMEM_LIMIT)


def _mm_body(*refs, nk, n_extra, epilogue):
    x_ref, w_ref = refs[0], refs[1]
    extras = refs[2:2 + n_extra]
    o_ref = refs[2 + n_extra]
    part = jnp.dot(x_ref[...], w_ref[...], preferred_element_type=F32)

    def finish(acc):
        if epilogue is not None:
            acc = epilogue(acc, *[e[...] for e in extras])
        o_ref[...] = acc.astype(o_ref.dtype)

    if nk == 1:
        finish(part)
        return
    acc_ref = refs[3 + n_extra]
    k = pl.program_id(2)

    @pl.when(k == 0)
    def _():
        acc_ref[...] = part

    @pl.when(k > 0)
    def _():
        acc_ref[...] += part

    @pl.when(k == nk - 1)
    def _():
        finish(acc_ref[...])


def _mm(x, w, *, out_dtype=F32, tm=None, tn=None, tk=None, epilogue=None, extras=()):
    m, k = x.shape
    n = w.shape[1]
    tm = _tile(m, tm or MM_TM, 16)
    tn = _tile(n, tn or MM_TN, LANES)
    tk = _tile(k, tk or MM_TK, LANES)
    nk = k // tk
    in_specs = [pl.BlockSpec((tm, tk), lambda i, j, kk: (i, kk)),
                pl.BlockSpec((tk, tn), lambda i, j, kk: (kk, j))]
    args = [x, w]
    for arr, kind in extras:
        if kind == 'tile':
            in_specs.append(pl.BlockSpec((tm, tn), lambda i, j, kk: (i, j)))
        elif kind == 'row':
            in_specs.append(pl.BlockSpec((1, tn), lambda i, j, kk: (0, j)))
        else:
            in_specs.append(pl.BlockSpec((tm, LANES), lambda i, j, kk: (i, 0)))
        args.append(arr)
    scratch = [pltpu.VMEM((tm, tn), F32)] if nk > 1 else []
    return pl.pallas_call(
        functools.partial(_mm_body, nk=nk, n_extra=len(extras), epilogue=epilogue),
        grid=(m // tm, n // tn, nk),
        in_specs=in_specs,
        out_specs=pl.BlockSpec((tm, tn), lambda i, j, kk: (i, j)),
        out_shape=jax.ShapeDtypeStruct((m, n), out_dtype),
        scratch_shapes=scratch,
        compiler_params=_cparams("parallel", "parallel", "arbitrary"),
    )(*args)


def _bmm_heads_body(x_ref, w_ref, o_ref):
    o_ref[...] = jnp.dot(x_ref[...], w_ref[0], preferred_element_type=F32).astype(o_ref.dtype)


def _bmm_heads(x, w, out_dtype=BF16):
    m = x.shape[0]
    h, din, dout = w.shape
    return pl.pallas_call(
        _bmm_heads_body,
        grid=(h,),
        in_specs=[pl.BlockSpec((m, din), lambda i: (0, i)),
                  pl.BlockSpec((1, din, dout), lambda i: (i, 0, 0))],
        out_specs=pl.BlockSpec((m, dout), lambda i: (0, i)),
        out_shape=jax.ShapeDtypeStruct((m, h * dout), out_dtype),
        compiler_params=_cparams("parallel"),
    )(x, w)


def _ln_body(*refs, n_add, alpha):
    x_ref = refs[0]
    adds = refs[1:1 + n_add]
    g_ref, b_ref, o_ref, obf_ref = refs[1 + n_add:]
    h = alpha * x_ref[...]
    for a in adds:
        h = h + a[...]
    mu = jnp.mean(h, axis=-1, keepdims=True)
    c = h - mu
    var = jnp.mean(c * c, axis=-1, keepdims=True)
    y = c * lax.rsqrt(var + LN_EPS) * g_ref[...] + b_ref[...]
    o_ref[...] = y
    obf_ref[...] = y.astype(BF16)


def _ln_res(x, adds, g, b, alpha):
    n, d = x.shape
    tm = _tile(n, LN_TM, 16)
    row = pl.BlockSpec((tm, d), lambda i: (i, 0))
    vec = pl.BlockSpec((1, d), lambda i: (0, 0))
    return pl.pallas_call(
        functools.partial(_ln_body, n_add=len(adds), alpha=alpha),
        grid=(n // tm,),
        in_specs=[row] * (1 + len(adds)) + [vec, vec],
        out_specs=[row, row],
        out_shape=[jax.ShapeDtypeStruct((n, d), F32), jax.ShapeDtypeStruct((n, d), BF16)],
        compiler_params=_cparams("parallel"),
    )(x, *adds, g.reshape(1, d), b.reshape(1, d))


def _gelu_tanh(y):
    return 0.5 * y * (1.0 + jnp.tanh(math.sqrt(2.0 / math.pi) * (y + 0.044715 * (y * y * y))))


def _s5_body(u_ref, bre_ref, bim_ref, cre_ref, cim_ref, lre_ref, lim_ref, d_ref, h0re_ref, h0im_ref,
             z_ref, zbf_ref, hre_ref, him_ref, car_re, car_im, *, t_rows, n_t):
    i = pl.program_id(2)

    @pl.when(i == 0)
    def _():
        car_re[...] = h0re_ref[0]
        car_im[...] = h0im_ref[0]

    u = u_ref[...]
    ub = u.astype(BF16)
    h_re = jnp.dot(ub, bre_ref[0], preferred_element_type=F32)
    h_im = jnp.dot(ub, bim_ref[0], preferred_element_type=F32)
    lr, li = lre_ref[0], lim_ref[0]
    cr, ci = car_re[...], car_im[...]
    row = lax.broadcasted_iota(jnp.int32, h_re.shape, 0)
    first = row == 0
    h_re = h_re + jnp.where(first, lr * cr - li * ci, 0.0)
    h_im = h_im + jnp.where(first, lr * ci + li * cr, 0.0)
    pr, pi = lr, li
    s = 1
    while s < t_rows:
        keep = row >= s
        sh_re = jnp.where(keep, pltpu.roll(h_re, s, 0), 0.0)
        sh_im = jnp.where(keep, pltpu.roll(h_im, s, 0), 0.0)
        h_re, h_im = h_re + pr * sh_re - pi * sh_im, h_im + pr * sh_im + pi * sh_re
        pr, pi = pr * pr - pi * pi, 2.0 * pr * pi
        s *= 2
    last_re = h_re[t_rows - 1:t_rows, :]
    last_im = h_im[t_rows - 1:t_rows, :]
    car_re[...] = last_re
    car_im[...] = last_im
    y = (jnp.dot(h_re.astype(BF16), cre_ref[0], preferred_element_type=F32)
         - jnp.dot(h_im.astype(BF16), cim_ref[0], preferred_element_type=F32))
    z = _gelu_tanh(y + d_ref[0] * u)
    z_ref[...] = z
    zbf_ref[...] = z.astype(BF16)

    @pl.when(i == n_t - 1)
    def _():
        hre_ref[0] = last_re
        him_ref[0] = last_im


def _s5_params(lam_re, lam_im, b_re, b_im, c_re, c_im, d, log_dt):
    g, p = lam_re.shape
    gs = b_re.shape[2]
    gpb = LANES // gs
    gb = g // gpb
    dt = jnp.exp(log_dt.astype(F32))[:, None]
    mag = jnp.exp(lam_re * dt)
    lb_re = mag * jnp.cos(lam_im * dt)
    lb_im = mag * jnp.sin(lam_im * dt)
    den = lam_re * lam_re + lam_im * lam_im
    q_re = ((lb_re - 1.0) * lam_re + lb_im * lam_im) / den
    q_im = (lb_im * lam_re - (lb_re - 1.0) * lam_im) / den
    bb_re = q_re[..., None] * b_re - q_im[..., None] * b_im
    bb_im = q_re[..., None] * b_im + q_im[..., None] * b_re
    eye = jnp.eye(gpb, dtype=F32)

    def pack_in(b):
        b = b.reshape(gb, gpb, p, gs).transpose(0, 1, 3, 2)
        return jnp.einsum('ab,xahp->xahbp', eye, b).reshape(gb, gpb * gs, gpb * p).astype(BF16)

    def pack_out(c):
        c = c.reshape(gb, gpb, gs, p).transpose(0, 1, 3, 2)
        return jnp.einsum('ab,xaph->xapbh', eye, c).reshape(gb, gpb * p, gpb * gs).astype(BF16)

    return dict(
        bre=pack_in(bb_re), bim=pack_in(bb_im), cre=pack_out(c_re.astype(F32)), cim=pack_out(c_im.astype(F32)),
        lre=lb_re.reshape(gb, 1, gpb * p), lim=lb_im.reshape(gb, 1, gpb * p),
        d=d.astype(F32).reshape(gb, 1, gpb * gs), gb=gb, s=gpb * p)


def _s5_scan(proj, row0, bsz, seq, prm, h0_re, h0_im):
    gb, s = prm['gb'], prm['s']
    t_rows = _tile(seq, S5_T, 8)
    n_t = seq // t_rows
    blk0 = row0 // t_rows
    assert row0 % t_rows == 0
    h0_re = h0_re.astype(F32).reshape(bsz * gb, 1, s)
    h0_im = h0_im.astype(F32).reshape(bsz * gb, 1, s)
    par = lambda shape: pl.BlockSpec((1,) + shape, lambda b, g, i: (g, 0, 0))
    st = pl.BlockSpec((1, 1, s), lambda b, g, i: (b * gb + g, 0, 0))
    zspec = pl.BlockSpec((t_rows, LANES), lambda b, g, i: (b * n_t + i, g))
    z, zbf, hre, him = pl.pallas_call(
        functools.partial(_s5_body, t_rows=t_rows, n_t=n_t),
        grid=(bsz, gb, n_t),
        in_specs=[pl.BlockSpec((t_rows, LANES), lambda b, g, i: (blk0 + b * n_t + i, g)),
                  par((LANES, s)), par((LANES, s)), par((s, LANES)), par((s, LANES)),
                  par((1, s)), par((1, s)), par((1, LANES)), st, st],
        out_specs=[zspec, zspec, st, st],
        out_shape=[jax.ShapeDtypeStruct((bsz * seq, gb * LANES), F32),
                   jax.ShapeDtypeStruct((bsz * seq, gb * LANES), BF16),
                   jax.ShapeDtypeStruct((bsz * gb, 1, s), F32),
                   jax.ShapeDtypeStruct((bsz * gb, 1, s), F32)],
        scratch_shapes=[pltpu.VMEM((1, s), F32), pltpu.VMEM((1, s), F32)],
        compiler_params=_cparams("parallel", "parallel", "arbitrary"),
    )(proj, prm['bre'], prm['bim'], prm['cre'], prm['cim'], prm['lre'], prm['lim'], prm['d'], h0_re, h0_im)
    return z, zbf, hre, him


def _softplus(x):
    return jnp.maximum(x, 0.0) + jnp.log1p(jnp.exp(-jnp.abs(x)))


def _rwkv_prep_body(pr_ref, pk_ref, pv_ref, pl_ref, sr_ref, sk_ref, sv_ref, sl_ref,
                    mr_ref, mk_ref, mv_ref, ml_ref, w0_ref, a0_ref, w2_ref, a2_ref, g2_ref,
                    r_ref, k_ref, v_ref, w_ref, a_ref, g_ref, cr, ck, cv, cl):
    i = pl.program_id(1)

    @pl.when(i == 0)
    def _():
        cr[...] = sr_ref[0]
        ck[...] = sk_ref[0]
        cv[...] = sv_ref[0]
        cl[...] = sl_ref[0]

    def mixed(p_ref, carry, mu_ref):
        x = p_ref[...]
        row = lax.broadcasted_iota(jnp.int32, x.shape, 0)
        prev = jnp.where(row == 0, carry[...], pltpu.roll(x, 1, 0))
        carry[...] = x[x.shape[0] - 1:, :]
        return x + (prev - x) * mu_ref[...]

    r_ref[...] = mixed(pr_ref, cr, mr_ref)
    k_ref[...] = mixed(pk_ref, ck, mk_ref)
    v_ref[...] = mixed(pv_ref, cv, mv_ref)
    lo = mixed(pl_ref, cl, ml_ref)
    tw = jnp.dot(jnp.tanh(lo).astype(BF16), w2_ref[...], preferred_element_type=F32)
    w_log = -_softplus(-(w0_ref[...] + tw)) - 0.5
    w_ref[...] = jnp.exp(-jnp.exp(w_log))
    a_ref[...] = jax.nn.sigmoid(a0_ref[...] + jnp.dot(lo.astype(BF16), a2_ref[...], preferred_element_type=F32))
    g_ref[...] = jnp.dot(jax.nn.sigmoid(lo).astype(BF16), g2_ref[...], preferred_element_type=F32)


def _rwkv_prep(proj, row0, bsz, seq, c, lp, col_r, shift, prm):
    t_rows = _tile(seq, RWKV_PREP_T, 8)
    n_t = seq // t_rows
    assert row0 % t_rows == 0 and col_r % c == 0 and (col_r + 3 * c) % lp == 0
    blk0 = row0 // t_rows
    cb = col_r // c
    lb = (col_r + 3 * c) // lp
    pin = lambda j: pl.BlockSpec((t_rows, c), lambda b, i: (blk0 + b * n_t + i, cb + j))
    sh = lambda w: pl.BlockSpec((1, 1, w), lambda b, i: (b, 0, 0))
    vec = lambda w: pl.BlockSpec((1, w), lambda b, i: (0, 0))
    mat = pl.BlockSpec((lp, c), lambda b, i: (0, 0))
    out = pl.BlockSpec((t_rows, c), lambda b, i: (b * n_t + i, 0))
    sds = jax.ShapeDtypeStruct((bsz * seq, c), F32)
    return pl.pallas_call(
        _rwkv_prep_body,
        grid=(bsz, n_t),
        in_specs=[pin(0), pin(1), pin(2),
                  pl.BlockSpec((t_rows, lp), lambda b, i: (blk0 + b * n_t + i, lb)),
                  sh(c), sh(c), sh(c), sh(lp), vec(c), vec(c), vec(c), vec(lp), vec(c), vec(c), mat, mat, mat],
        out_specs=[out] * 6,
        out_shape=[sds] * 6,
        scratch_shapes=[pltpu.VMEM((1, c), F32)] * 3 + [pltpu.VMEM((1, lp), F32)],
        compiler_params=_cparams("parallel", "arbitrary"),
    )(proj, proj, proj, proj, shift[0], shift[1], shift[2], shift[3],
      prm['mu_r'], prm['mu_k'], prm['mu_v'], prm['mu_l'], prm['w0'], prm['a0'], prm['w2'], prm['a2'], prm['g2'])


def _rwkv_scan_body(r_ref, k_ref, v_ref, w_ref, a_ref, kkp_ref, kap_ref, rkp_ref, lnw_ref, lnb_ref, s0_ref,
                    y_ref, sout_ref, st, kk_s, kka_s, km_s, wr_s, c1_s, c2_s, yr_s, *, t_rows, n_t, hd):
    i = pl.program_id(1)

    @pl.when(i == 0)
    def _():
        st[...] = s0_ref[...]

    r = r_ref[...]
    kraw = k_ref[...]
    a = a_ref[...]
    kk = kraw * kkp_ref[...][None]
    kk = kk * lax.rsqrt(jnp.maximum(jnp.sum(kk * kk, axis=1, keepdims=True), 1e-24))
    km = kraw * (1.0 + (a - 1.0) * kap_ref[...][None])
    kka = kk * a
    kk_s[...] = kk
    kka_s[...] = kka
    km_s[...] = km
    wr_s[...] = w_ref[...] * r
    c1_s[...] = jnp.sum(kka * r, axis=1, keepdims=True)
    c2_s[...] = jnp.sum(km * r, axis=1, keepdims=True)

    def step(t, carry):
        sk = jnp.zeros((hd, LANES), F32)
        yq = jnp.zeros((hd, LANES), F32)
        for j in range(hd):
            sj = st[j]
            sk = sk + sj * kk_s[t, pl.ds(j, 1), :]
            yq = yq + sj * wr_s[t, pl.ds(j, 1), :]
        v_t = v_ref[t]
        for j in range(hd):
            st[j] = (st[j] * w_ref[t, pl.ds(j, 1), :] - kka_s[t, pl.ds(j, 1), :] * sk
                     + km_s[t, pl.ds(j, 1), :] * v_t)
        yr_s[t] = yq - sk * c1_s[t] + v_t * c2_s[t]
        return carry

    lax.fori_loop(0, t_rows, step, 0)
    y = yr_s[...]
    ym = jnp.mean(y, axis=1, keepdims=True)
    yc = y - ym
    yv = jnp.mean(yc * yc, axis=1, keepdims=True)
    yn = yc * lax.rsqrt(yv + RWKV_GN_EPS) * lnw_ref[...][None] + lnb_ref[...][None]
    bonus = jnp.sum(r * km * rkp_ref[...][None], axis=1, keepdims=True) * v_ref[...]
    y_ref[...] = yn + bonus

    @pl.when(i == n_t - 1)
    def _():
        sout_ref[...] = st[...]


def _rwkv_scan(r, k, v, w, a, prm, s0):
    seq, hd, bh = r.shape
    t_rows = _tile(seq, RWKV_SCAN_T, 1)
    n_t = seq // t_rows
    n_lb = bh // LANES
    seq_spec = pl.BlockSpec((t_rows, hd, LANES), lambda l, i: (i, 0, l))
    par = pl.BlockSpec((hd, LANES), lambda l, i: (0, l))
    st_spec = pl.BlockSpec((hd, hd, LANES), lambda l, i: (0, 0, l))
    big = pltpu.VMEM((t_rows, hd, LANES), F32)
    small = pltpu.VMEM((t_rows, 1, LANES), F32)
    return pl.pallas_call(
        functools.partial(_rwkv_scan_body, t_rows=t_rows, n_t=n_t, hd=hd),
        grid=(n_lb, n_t),
        in_specs=[seq_spec] * 5 + [par] * 5 + [st_spec],
        out_specs=[seq_spec, st_spec],
        out_shape=[jax.ShapeDtypeStruct((seq, hd, bh), F32), jax.ShapeDtypeStruct((hd, hd, bh), F32)],
        scratch_shapes=[pltpu.VMEM((hd, hd, LANES), F32), big, big, big, big, small, small, big],
        compiler_params=_cparams("parallel", "arbitrary"),
    )(r, k, v, w, a, prm['k_k'], prm['k_a'], prm['r_k'], prm['ln_w'], prm['ln_b'], s0)


def _mul_cast_body(y_ref, g_ref, o_ref):
    o_ref[...] = (y_ref[...] * g_ref[...]).astype(o_ref.dtype)


def _mul_cast(y, g):
    n, d = y.shape
    tm = _tile(n, LN_TM * 2, 16)
    spec = pl.BlockSpec((tm, d), lambda i: (i, 0))
    return pl.pallas_call(
        _mul_cast_body, grid=(n // tm,), in_specs=[spec, spec], out_specs=spec,
        out_shape=jax.ShapeDtypeStruct((n, d), BF16), compiler_params=_cparams("parallel"),
    )(y, g)


def _rope_lanes(x, cos, sin, half):
    lane = lax.broadcasted_iota(jnp.int32, x.shape, 1)
    swapped = jnp.where(lane < half, pltpu.roll(x, LANES - half, 1), pltpu.roll(x, half, 1))
    return x * cos + swapped * sin


def _mla_prep_body(p_ref, qg_ref, kg_ref, cos_ref, sin_ref, cq_ref, ckv_ref, ckvb_ref, kr_ref, krb_ref,
                   *, ql, kl, half):
    x = p_ref[...]
    cq = x[:, :ql]
    cq_ref[...] = (cq * lax.rsqrt(jnp.mean(cq * cq, axis=-1, keepdims=True) + RMS_EPS) * qg_ref[...]).astype(BF16)
    ckv = x[:, ql:ql + kl]
    ckv = ckv * lax.rsqrt(jnp.mean(ckv * ckv, axis=-1, keepdims=True) + RMS_EPS) * kg_ref[...]
    ckv_ref[...] = ckv
    ckvb_ref[...] = ckv.astype(BF16)
    kr = _rope_lanes(x[:, ql + kl:], cos_ref[...], sin_ref[...], half)
    kr_ref[...] = kr
    krb_ref[...] = kr.astype(BF16)


def _mla_prep(proj, q_norm, kv_norm, cos, sin, ql, kl, half):
    n, width = proj.shape
    tm = _tile(n, LN_TM, 16)
    row = lambda w: pl.BlockSpec((tm, w), lambda i: (i, 0))
    vec = lambda w: pl.BlockSpec((1, w), lambda i: (0, 0))
    return pl.pallas_call(
        functools.partial(_mla_prep_body, ql=ql, kl=kl, half=half),
        grid=(n // tm,),
        in_specs=[row(width), vec(ql), vec(kl), row(LANES), row(LANES)],
        out_specs=[row(ql), row(kl), row(kl), row(LANES), row(LANES)],
        out_shape=[jax.ShapeDtypeStruct((n, ql), BF16), jax.ShapeDtypeStruct((n, kl), F32),
                   jax.ShapeDtypeStruct((n, kl), BF16), jax.ShapeDtypeStruct((n, LANES), F32),
                   jax.ShapeDtypeStruct((n, LANES), BF16)],
        compiler_params=_cparams("parallel"),
    )(proj, q_norm.reshape(1, ql), kv_norm.reshape(1, kl), cos, sin)


def _online_softmax_update(s, v_blk, m_s, l_s, acc_s):
    m_prev = m_s[...]
    m_new = jnp.maximum(m_prev, jnp.max(s, axis=1, keepdims=True))
    alpha = jnp.exp(m_prev - m_new)
    p = jnp.exp(s - m_new)
    l_s[...] = alpha * l_s[...] + jnp.sum(p, axis=1, keepdims=True)
    acc_s[...] = alpha * acc_s[...] + jnp.dot(p.astype(BF16), v_blk, preferred_element_type=F32)
    m_s[...] = m_new


_NT = (((1,), (1,)), ((), ()))


def _attn_prompt_body(qn_ref, qr_ref, kn_ref, kr_ref, v_ref, o_ref, m_s, l_s, acc_s, *, tq, scale):
    i = pl.program_id(2)
    m_s[...] = jnp.full(m_s.shape, MASK_VALUE, F32)
    l_s[...] = jnp.zeros(l_s.shape, F32)
    acc_s[...] = jnp.zeros(acc_s.shape, F32)
    qn = qn_ref[...]
    qr = qr_ref[...]

    def scores(j):
        off = pl.multiple_of(j * tq, tq)
        s = lax.dot_general(qn, kn_ref[pl.ds(off, tq), :], _NT, preferred_element_type=F32)
        s = s + lax.dot_general(qr, kr_ref[pl.ds(off, tq), :], _NT, preferred_element_type=F32)
        return s * scale, v_ref[pl.ds(off, tq), :]

    def full_tile(j, carry):
        s, v_blk = scores(j)
        _online_softmax_update(s, v_blk, m_s, l_s, acc_s)
        return carry

    lax.fori_loop(0, i, full_tile, 0)
    s, v_blk = scores(i)
    row = lax.broadcasted_iota(jnp.int32, s.shape, 0)
    col = lax.broadcasted_iota(jnp.int32, s.shape, 1)
    s = jnp.where(col // CHUNK <= row // CHUNK, s, MASK_VALUE)
    _online_softmax_update(s, v_blk, m_s, l_s, acc_s)
    o_ref[...] = (acc_s[...] / l_s[...]).astype(o_ref.dtype)


def _attn_prompt(qn, qr, kn, kr, v, bsz, seq, heads, scale):
    tq = _tile(seq, ATTN_TQ, max(CHUNK, 16))
    assert tq % CHUNK == 0
    n_q = seq // tq
    qspec = pl.BlockSpec((tq, LANES), lambda b, h, i: (b * n_q + i, h))
    kspec = pl.BlockSpec((seq, LANES), lambda b, h, i: (b, h))
    return pl.pallas_call(
        functools.partial(_attn_prompt_body, tq=tq, scale=scale),
        grid=(bsz, heads, n_q),
        in_specs=[qspec, qspec, kspec, pl.BlockSpec((seq, LANES), lambda b, h, i: (b, 0)), kspec],
        out_specs=qspec,
        out_shape=jax.ShapeDtypeStruct((bsz * seq, heads * LANES), BF16),
        scratch_shapes=[pltpu.VMEM((tq, 1), F32), pltpu.VMEM((tq, 1), F32), pltpu.VMEM((tq, LANES), F32)],
        compiler_params=_cparams("parallel", "parallel", "arbitrary"),
    )(qn, qr, kn, kr, v)


def _attn_latent_body(ql_ref, qr_ref, kv_ref, kr_ref, o_ref, m_s, l_s, acc_s,
                      *, tq, tk, n_k, heads, q_pos0, n_keys, scale):
    i = pl.program_id(1)
    m_s[...] = jnp.full(m_s.shape, MASK_VALUE, F32)
    l_s[...] = jnp.zeros(l_s.shape, F32)
    acc_s[...] = jnp.zeros(acc_s.shape, F32)
    ql = ql_ref[...]
    qr = qr_ref[...]

    def tile(j, carry):
        off = pl.multiple_of(j * tk, tk)
        kv = kv_ref[pl.ds(off, tk), :]
        s = lax.dot_general(ql, kv, _NT, preferred_element_type=F32)
        s = s + lax.dot_general(qr, kr_ref[pl.ds(off, tk), :], _NT, preferred_element_type=F32)
        row = lax.broadcasted_iota(jnp.int32, s.shape, 0)
        col = lax.broadcasted_iota(jnp.int32, s.shape, 1)
        q_pos = q_pos0 + (i * tq + row) // heads
        k_pos = off + col
        visible = (k_pos // CHUNK <= q_pos // CHUNK) & (k_pos < n_keys)
        s = jnp.where(visible, s * scale, MASK_VALUE)
        _online_softmax_update(s, kv, m_s, l_s, acc_s)
        return carry

    lax.fori_loop(0, n_k, tile, 0)
    o_ref[...] = (acc_s[...] / l_s[...]).astype(o_ref.dtype)


def _attn_latent(q_lat, q_rope, kv_all, kr_all, bsz, rows, heads, q_pos0, n_keys, scale):
    kl = q_lat.shape[1]
    nkp = kv_all.shape[0] // bsz
    tq = _tile(rows, ATTN_TQ, 16)
    tk = _tile(nkp, ATTN_TQ, 16)
    n_q = rows // tq
    return pl.pallas_call(
        functools.partial(_attn_latent_body, tq=tq, tk=tk, n_k=nkp // tk, heads=heads,
                          q_pos0=q_pos0, n_keys=n_keys, scale=scale),
        grid=(bsz, n_q),
        in_specs=[pl.BlockSpec((tq, kl), lambda b, i: (b * n_q + i, 0)),
                  pl.BlockSpec((tq, LANES), lambda b, i: (b * n_q + i, 0)),
                  pl.BlockSpec((nkp, kl), lambda b, i: (b, 0)),
                  pl.BlockSpec((nkp, LANES), lambda b, i: (b, 0))],
        out_specs=pl.BlockSpec((tq, kl), lambda b, i: (b * n_q + i, 0)),
        out_shape=jax.ShapeDtypeStruct((bsz * rows, kl), BF16),
        scratch_shapes=[pltpu.VMEM((tq, 1), F32), pltpu.VMEM((tq, 1), F32), pltpu.VMEM((tq, kl), F32)],
        compiler_params=_cparams("parallel", "arbitrary"),
    )(q_lat, q_rope, kv_all, kr_all)


def _router_body(x_ref, wh_ref, wl_ref, o_ref):
    x = x_ref[...]
    xh = x.astype(BF16)
    xl = (x - xh.astype(F32)).astype(BF16)
    wh = wh_ref[...]
    acc = jnp.dot(xh, wh, preferred_element_type=F32)
    acc = acc + jnp.dot(xl, wh, preferred_element_type=F32)
    acc = acc + jnp.dot(xh, wl_ref[...], preferred_element_type=F32)
    o_ref[...] = jax.nn.sigmoid(acc)


def _router(x, w):
    n, d = x.shape
    e = w.shape[1]
    tm = _tile(n, ROUTER_TM, 8)
    wh = w.astype(BF16)
    wl = (w - wh.astype(F32)).astype(BF16)
    wspec = pl.BlockSpec((d, e), lambda i: (0, 0))
    return pl.pallas_call(
        _router_body, grid=(n // tm,),
        in_specs=[pl.BlockSpec((tm, d), lambda i: (i, 0)), wspec, wspec],
        out_specs=pl.BlockSpec((tm, e), lambda i: (i, 0)),
        out_shape=jax.ShapeDtypeStruct((n, e), F32),
        compiler_params=_cparams("parallel"),
    )(x, wh, wl)


def _moe_block(b, be_ref, nu_ref):
    bb = jnp.minimum(b, nu_ref[0] - 1)
    changed = (b == 0) | (be_ref[bb] != be_ref[jnp.maximum(bb - 1, 0)])
    return b < nu_ref[0], changed


def _moe_up_body(be_ref, nu_ref, x_ref, wg_ref, wu_ref, o_ref, wgb, wub):
    active, changed = _moe_block(pl.program_id(1), be_ref, nu_ref)

    @pl.when(active & changed)
    def _():
        wgb[...] = wg_ref[0].astype(BF16)
        wub[...] = wu_ref[0].astype(BF16)

    @pl.when(active)
    def _():
        x = x_ref[...]
        h1 = jnp.dot(x, wgb[...], preferred_element_type=F32)
        h2 = jnp.dot(x, wub[...], preferred_element_type=F32)
        o_ref[...] = (h1 * jax.nn.sigmoid(h1) * h2).astype(o_ref.dtype)


def _moe_down_body(be_ref, nu_ref, h_ref, wd_ref, sw_ref, o_ref, wdb):
    active, changed = _moe_block(pl.program_id(1), be_ref, nu_ref)

    @pl.when(active & changed)
    def _():
        wdb[...] = wd_ref[0].astype(BF16)

    @pl.when(active)
    def _():
        o_ref[...] = jnp.dot(h_ref[...], wdb[...], preferred_element_type=F32) * sw_ref[...]


def _expert_ffn(xs, slot_w, blk_expert, n_used, w_gate, w_up, w_down, tm):
    p, d = xs.shape
    ff = w_gate.shape[2]
    n_blk = p // tm
    tf = _tile(ff, MOE_TF, LANES)
    tn = _tile(d, MOE_TN, LANES)
    blk = lambda b, nu: jnp.minimum(b, nu[0] - 1)
    hid = pl.pallas_call(
        _moe_up_body,
        grid_spec=pltpu.PrefetchScalarGridSpec(
            num_scalar_prefetch=2, grid=(ff // tf, n_blk),
            in_specs=[pl.BlockSpec((tm, d), lambda f, b, be, nu: (blk(b, nu), 0)),
                      pl.BlockSpec((1, d, tf), lambda f, b, be, nu: (be[blk(b, nu)], 0, f)),
                      pl.BlockSpec((1, d, tf), lambda f, b, be, nu: (be[blk(b, nu)], 0, f))],
            out_specs=pl.BlockSpec((tm, tf), lambda f, b, be, nu: (blk(b, nu), f)),
            scratch_shapes=[pltpu.VMEM((d, tf), BF16), pltpu.VMEM((d, tf), BF16)]),
        out_shape=jax.ShapeDtypeStruct((p, ff), BF16),
        compiler_params=_cparams("arbitrary", "arbitrary"),
    )(blk_expert, n_used, xs, w_gate, w_up)
    return pl.pallas_call(
        _moe_down_body,
        grid_spec=pltpu.PrefetchScalarGridSpec(
            num_scalar_prefetch=2, grid=(d // tn, n_blk),
            in_specs=[pl.BlockSpec((tm, ff), lambda n, b, be, nu: (blk(b, nu), 0)),
                      pl.BlockSpec((1, ff, tn), lambda n, b, be, nu: (be[blk(b, nu)], 0, n)),
                      pl.BlockSpec((tm, 1), lambda n, b, be, nu: (blk(b, nu), 0))],
            out_specs=pl.BlockSpec((tm, tn), lambda n, b, be, nu: (blk(b, nu), n)),
            scratch_shapes=[pltpu.VMEM((ff, tn), BF16)]),
        out_shape=jax.ShapeDtypeStruct((p, d), F32),
        compiler_params=_cparams("arbitrary", "arbitrary"),
    )(blk_expert, n_used, hid, w_down, slot_w)


def _moe_ffn(x, x_bf, layer, w_router, b_router, w_gate, w_up, w_down, ws_gate, ws_up, ws_down):
    n_tok, d = x.shape
    n_exp = w_router.shape[2]
    scores = _router(x, w_router[layer].astype(F32))
    biased = scores + b_router[layer].astype(F32)
    grp_score = lax.top_k(biased.reshape(n_tok, N_GROUP, n_exp // N_GROUP), 2)[0].sum(-1)
    _, grp_idx = lax.top_k(grp_score, TOPK_GROUP)
    grp_keep = jnp.any(grp_idx[:, :, None] == jnp.arange(N_GROUP)[None, None, :], axis=1)
    keep = jnp.repeat(grp_keep, n_exp // N_GROUP, axis=-1)
    _, top_idx = lax.top_k(jnp.where(keep, biased, -jnp.inf), TOP_K)
    top_w = jnp.take_along_axis(scores, top_idx, axis=-1)
    top_w = top_w / (top_w.sum(-1, keepdims=True) + 1e-20) * ROUTED_SCALE
    n_asg = n_tok * TOP_K
    tm = _tile(n_tok, MOE_TM, 16)
    n_blk = -(-n_asg // tm) + n_exp
    flat_e = top_idx.reshape(-1)
    order = jnp.argsort(flat_e)
    e_sorted = flat_e[order]
    counts = jnp.bincount(flat_e, length=n_exp)
    padded = (counts + tm - 1) // tm * tm
    pad_end = jnp.cumsum(padded)
    pad_start = pad_end - padded
    start = jnp.cumsum(counts) - counts
    dest = (pad_start[e_sorted] + jnp.arange(n_asg) - start[e_sorted]).astype(jnp.int32)
    slot_tok = jnp.full((n_blk * tm,), n_tok, jnp.int32).at[dest].set((order // TOP_K).astype(jnp.int32))
    slot_w = jnp.zeros((n_blk * tm,), F32).at[dest].set(top_w.reshape(-1)[order])
    blk_expert = jnp.minimum(jnp.searchsorted(pad_end, jnp.arange(n_blk) * tm, side='right'),
                             n_exp - 1).astype(jnp.int32)
    n_used = (pad_end[-1] // tm).astype(jnp.int32).reshape(1)
    slot_of_asg = jnp.zeros((n_asg,), jnp.int32).at[order].set(dest)
    x_pad = jnp.concatenate([x_bf, jnp.zeros((1, d), BF16)], axis=0)
    xs = x_pad[slot_tok]
    y_slots = _expert_ffn(xs, slot_w[:, None], blk_expert, n_used, w_gate[layer], w_up[layer], w_down[layer], tm)
    routed = y_slots[slot_of_asg].reshape(n_tok, TOP_K, d).sum(axis=1)
    n_sblk = n_tok // tm
    shared = _expert_ffn(x_bf, jnp.ones((n_tok, 1), F32), jnp.zeros((n_sblk,), jnp.int32),
                         jnp.full((1,), n_sblk, jnp.int32), ws_gate[layer][None], ws_up[layer][None],
                         ws_down[layer][None], tm)
    return routed, shared


def _to_scan_layout(t, bsz, seq, heads, hd, bh_pad):
    t = t.reshape(bsz, seq, heads, hd).transpose(1, 3, 0, 2).reshape(seq, hd, bsz * heads)
    return jnp.pad(t, ((0, 0), (0, 0), (0, bh_pad - bsz * heads)))


def _rope_tables(pos, half):
    inv = jnp.exp(-math.log(ROPE_THETA) * jnp.arange(half, dtype=F32) / half)
    ang = pos.astype(F32)[:, None] * inv[None, :]
    cos, sin = jnp.cos(ang), jnp.sin(ang)
    pad = LANES - 2 * half
    ones = jnp.ones((pos.shape[0], pad), F32)
    zeros = jnp.zeros((pos.shape[0], pad), F32)
    return jnp.concatenate([cos, cos, ones], 1), jnp.concatenate([-sin, sin, zeros], 1)


def kernel(x_prompt, x_sample, state_s5_re, state_s5_im, state_rwkv, state_rwkv_shift, cache_c_kv, cache_k_rope, l0_w_in, s5_lambda_re, s5_lambda_im, s5_b_re, s5_b_im, s5_c_re, s5_c_im, s5_d, s5_log_dt, s5_w_glu, s5_b_glu, rwkv_mu, rwkv_w0, rwkv_w2, rwkv_a0, rwkv_a2, rwkv_g2, rwkv_k_k, rwkv_k_a, rwkv_r_k, rwkv_ln_w, rwkv_ln_b, l0_w_out, l1_w_in, mla_q_norm, mla_w_uq, mla_kv_norm, mla_w_uk, mla_w_uv, l1_w_out, ln_mix_g, ln_mix_b, ln_ffn_g, ln_ffn_b, moe_w_router, moe_b_router, moe_w_gate, moe_w_up, moe_w_down, moe_ws_gate, moe_ws_up, moe_ws_down):
    bp, lp_, d = x_prompt.shape
    bs, ls, _ = x_sample.shape
    n_p, n_s = bp * lp_, bs * ls
    depth = ln_mix_g.shape[0]
    alpha = (2 * depth) ** 0.25
    groups = (("p", bp, lp_, 0), ("s", bs, ls, n_p))

    x = jnp.concatenate([x_prompt.reshape(n_p, d), x_sample.reshape(n_s, d)], axis=0).astype(F32)
    x_bf = x.astype(BF16)
    moe_p = (moe_w_router, moe_b_router, moe_w_gate, moe_w_up, moe_w_down, moe_ws_gate, moe_ws_up, moe_ws_down)

    s5_g, s5_p = s5_lambda_re.shape
    ws = s5_g * s5_b_re.shape[2]
    heads_r, hd = rwkv_r_k.shape
    c = heads_r * hd
    lora_w, lora_a, lora_g = rwkv_w2.shape[0], rwkv_a2.shape[0], rwkv_g2.shape[0]
    lora = lora_w + lora_a + lora_g
    lp = -(-lora // LANES) * LANES
    proj_w = ws + 3 * c + lora
    w_in = jnp.pad(l0_w_in, ((0, 0), (0, lp - lora))).astype(BF16)
    proj = _mm(x_bf, w_in, tn=512)
    s5p = _s5_params(s5_lambda_re.astype(F32), s5_lambda_im.astype(F32), s5_b_re.astype(F32),
                     s5_b_im.astype(F32), s5_c_re, s5_c_im, s5_d, s5_log_dt)
    mu = jnp.pad(rwkv_mu.astype(F32), (0, lp - lora)).reshape(1, -1)
    row_pad = lambda m, at: jnp.pad(m.astype(F32), ((at, lp - at - m.shape[0]), (0, 0))).astype(BF16)
    prep_p = dict(mu_r=mu[:, :c], mu_k=mu[:, c:2 * c], mu_v=mu[:, 2 * c:3 * c], mu_l=mu[:, 3 * c:],
                  w0=rwkv_w0.astype(F32).reshape(1, c), a0=rwkv_a0.astype(F32).reshape(1, c),
                  w2=row_pad(rwkv_w2, 0), a2=row_pad(rwkv_a2, lora_w), g2=row_pad(rwkv_g2, lora_w + lora_a))

    zs, zbfs, yrs, outs0 = [], [], [], {}
    for name, bsz, seq, row0 in groups:
        if name == "p":
            h0re = jnp.zeros((bsz, s5_g, s5_p), F32)
            h0im = h0re
            s0 = jnp.zeros((bsz, heads_r, hd, hd), F32)
            shift0 = jnp.zeros((bsz, proj_w - ws), F32)
        else:
            h0re, h0im, s0, shift0 = state_s5_re, state_s5_im, state_rwkv.astype(F32), state_rwkv_shift.astype(F32)
        z, zbf, hre, him = _s5_scan(proj, row0, bsz, seq, s5p, h0re, h0im)
        zs.append(z)
        zbfs.append(zbf)
        shift0 = jnp.pad(shift0, ((0, 0), (0, lp - lora)))[:, None, :]
        shifts = (shift0[..., :c], shift0[..., c:2 * c], shift0[..., 2 * c:3 * c], shift0[..., 3 * c:])
        r, k, v, w, a, g = _rwkv_prep(proj, row0, bsz, seq, c, lp, ws, shifts, prep_p)
        bh = bsz * heads_r
        bh_pad = -(-bh // LANES) * LANES
        lay = lambda t: _to_scan_layout(t, bsz, seq, heads_r, hd, bh_pad)
        par = lambda t: jnp.pad(jnp.tile(t.astype(F32).reshape(heads_r, hd).T, (1, bsz)), ((0, 0), (0, bh_pad - bh)))
        scan_p = dict(k_k=par(rwkv_k_k), k_a=par(rwkv_k_a), r_k=par(rwkv_r_k), ln_w=par(rwkv_ln_w), ln_b=par(rwkv_ln_b))
        s0l = jnp.pad(s0.transpose(3, 2, 0, 1).reshape(hd, hd, bh), ((0, 0), (0, 0), (0, bh_pad - bh)))
        y_scan, s_last = _rwkv_scan(lay(r), lay(k), lay(v), lay(w), lay(a), scan_p, s0l)
        y_nat = y_scan[:, :, :bh].reshape(seq, hd, bsz, heads_r).transpose(2, 0, 3, 1).reshape(bsz * seq, c)
        yrs.append(_mul_cast(y_nat, g))
        last = proj[row0:row0 + bsz * seq].reshape(bsz, seq, -1)[:, -1, ws:proj_w]
        outs0[name] = (hre.reshape(bsz, s5_g, s5_p).astype(state_s5_re.dtype),
                       him.reshape(bsz, s5_g, s5_p).astype(state_s5_im.dtype),
                       s_last[:, :, :bh].reshape(hd, hd, bsz, heads_r).transpose(2, 3, 1, 0).astype(state_rwkv.dtype),
                       last.astype(state_rwkv_shift.dtype))
    z = jnp.concatenate(zs, axis=0)
    zbf = jnp.concatenate(zbfs, axis=0)
    y_a = _mm(zbf, s5_w_glu.astype(BF16), out_dtype=BF16,
              epilogue=lambda acc, zt, b: zt * jax.nn.sigmoid(acc + b),
              extras=((z, 'tile'), (s5_b_glu.astype(F32).reshape(1, ws), 'row')))
    mixed = jnp.concatenate([y_a, jnp.concatenate(yrs, axis=0)], axis=1)
    mix = _mm(mixed, l0_w_out.astype(BF16))
    x, x_bf = _ln_res(x, [mix], ln_mix_g[0].astype(F32), ln_mix_b[0].astype(F32), alpha)
    routed, shared = _moe_ffn(x, x_bf, 0, *moe_p)
    x, x_bf = _ln_res(x, [routed, shared], ln_ffn_g[0].astype(F32), ln_ffn_b[0].astype(F32), alpha)

    ql, heads_a, qd = mla_w_uq.shape
    kl, _, nope = mla_w_uk.shape
    vd = mla_w_uv.shape[2]
    rope = qd - nope
    half = rope // 2
    past = cache_c_kv.shape[1]
    assert nope == LANES and vd == LANES and rope <= LANES
    scale = float(qd) ** -0.5
    w1 = jnp.pad(l1_w_in, ((0, 0), (0, LANES - rope))).astype(BF16)
    proj1 = _mm(x_bf, w1, tm=512, tn=ql + kl + LANES)
    pos = jnp.concatenate([jnp.tile(jnp.arange(lp_, dtype=jnp.int32), bp),
                           jnp.tile(past + jnp.arange(ls, dtype=jnp.int32), bs)])
    cos, sin = _rope_tables(pos, half)
    cqn, ckv, ckv_bf, kr, kr_bf = _mla_prep(proj1, mla_q_norm.astype(F32), mla_kv_norm.astype(F32), cos, sin, ql, kl, half)
    w_qn = mla_w_uq[:, :, :nope].reshape(ql, heads_a * nope).astype(BF16)
    w_qr = jnp.pad(mla_w_uq[:, :, nope:], ((0, 0), (0, 0), (0, LANES - rope))).reshape(ql, heads_a * LANES).astype(BF16)
    q_nope = _mm(cqn, w_qn, out_dtype=BF16)

    def rope_epilogue(acc, cos_t, sin_t):
        return jnp.concatenate([_rope_lanes(acc[:, j * LANES:(j + 1) * LANES], cos_t, sin_t, half)
                                for j in range(acc.shape[1] // LANES)], axis=1)

    q_rope = _mm(cqn, w_qr, out_dtype=BF16, epilogue=rope_epilogue, extras=((cos, 'rowtile'), (sin, 'rowtile')))
    k_nope = _mm(ckv_bf[:n_p], mla_w_uk.reshape(kl, heads_a * nope).astype(BF16), out_dtype=BF16)
    v_p = _mm(ckv_bf[:n_p], mla_w_uv.reshape(kl, heads_a * vd).astype(BF16), out_dtype=BF16)
    o_p = _attn_prompt(q_nope[:n_p], q_rope[:n_p], k_nope, kr_bf[:n_p], v_p, bp, lp_, heads_a, scale)
    q_lat = _bmm_heads(q_nope[n_p:], mla_w_uk.transpose(1, 2, 0).astype(BF16))
    n_keys = past + ls
    nkp = -(-n_keys // ATTN_TQ) * ATTN_TQ
    kv_all = jnp.concatenate([cache_c_kv.astype(F32), ckv[n_p:].reshape(bs, ls, kl)], axis=1)
    kr_all = jnp.concatenate([jnp.pad(cache_k_rope.astype(F32), ((0, 0), (0, 0), (0, LANES - rope))),
                              kr[n_p:].reshape(bs, ls, LANES)], axis=1)
    padk = lambda t: jnp.pad(t, ((0, 0), (0, nkp - n_keys), (0, 0))).astype(BF16).reshape(bs * nkp, -1)
    o_lat = _attn_latent(q_lat.reshape(n_s * heads_a, kl), q_rope[n_p:].reshape(n_s * heads_a, LANES),
                         padk(kv_all), padk(kr_all), bs, ls * heads_a, heads_a, past, n_keys, scale)
    o_s = _bmm_heads(o_lat.reshape(n_s, heads_a * kl), mla_w_uv.transpose(1, 0, 2).astype(BF16))
    mix = _mm(jnp.concatenate([o_p, o_s], axis=0), l1_w_out.astype(BF16))
    x, x_bf = _ln_res(x, [mix], ln_mix_g[1].astype(F32), ln_mix_b[1].astype(F32), alpha)
    routed, shared = _moe_ffn(x, x_bf, 1, *moe_p)
    x, _ = _ln_res(x, [routed, shared], ln_ffn_g[1].astype(F32), ln_ffn_b[1].astype(F32), alpha)

    dt = x_prompt.dtype
    y_prompt = x[:n_p].reshape(bp, lp_, d).astype(dt)
    y_sample = x[n_p:].reshape(bs, ls, d).astype(dt)
    ckv_p = ckv[:n_p].reshape(bp, lp_, kl).astype(dt)
    ckv_s = ckv[n_p:].reshape(bs, ls, kl).astype(dt)
    kr_p = kr[:n_p, :rope].reshape(bp, lp_, rope).astype(dt)
    kr_s = kr[n_p:, :rope].reshape(bs, ls, rope).astype(dt)
    return (y_prompt, y_sample, *outs0["p"], ckv_p, kr_p, *outs0["s"], ckv_s, kr_s)
```

```python
import functools
import math

import jax
import jax.numpy as jnp
from jax import lax
from jax.experimental import pallas as pl
from jax.experimental.pallas import tpu as pltpu

F32 = jnp.float32
BF16 = jnp.bfloat16

CHUNK = 64
LN_EPS = 1e-5
RMS_EPS = 1e-6
RWKV_GN_EPS = 64e-5
ROPE_THETA = 10000.0
TOP_K = 8
N_GROUP = 8
TOPK_GROUP = 4
ROUTED_SCALE = 2.5
MASK_VALUE = -1e30

LANES = 128
V7X_VMEM_BYTES = 64 * 2**20
VMEM_LIMIT = V7X_VMEM_BYTES - 8 * 2**20

MM_TM = 1024
MM_TN = 1024
MM_TK = 4096
LN_TM = 256
S5_T = 512
RWKV_PREP_T = 256
RWKV_SCAN_T = 64
ATTN_TQ = 512
MOE_TM = 256
MOE_TF = 512
MOE_TN = 2048
ROUTER_TM = 512


def _tile(n, target, align):
    if n <= target:
        return n
    t = (target // align) * align
    while t >= align:
        if n % t == 0:
            return t
        t -= align
    return n


def _cparams(*sems):
    return pltpu.CompilerParams(dimension_semantics=sems, vmem_limit_bytes=VMEM_LIMIT)


def _mm_body(*refs, nk, n_extra, epilogue):
    x_ref, w_ref = refs[0], refs[1]
    extras = refs[2:2 + n_extra]
    o_ref = refs[2 + n_extra]
    part = jnp.dot(x_ref[...], w_ref[...], preferred_element_type=F32)

    def finish(acc):
        if epilogue is not None:
            acc = epilogue(acc, *[e[...] for e in extras])
        o_ref[...] = acc.astype(o_ref.dtype)

    if nk == 1:
        finish(part)
        return
    acc_ref = refs[3 + n_extra]
    k = pl.program_id(2)

    @pl.when(k == 0)
    def _():
        acc_ref[...] = part

    @pl.when(k > 0)
    def _():
        acc_ref[...] += part

    @pl.when(k == nk - 1)
    def _():
        finish(acc_ref[...])


def _mm(x, w, *, out_dtype=F32, tm=None, tn=None, tk=None, epilogue=None, extras=(), m=None, n=None, name="mm"):
    k = x.shape[1]
    m = m or x.shape[0]
    n = n or w.shape[1]
    tm = _tile(m, tm or MM_TM, 16)
    tn = _tile(n, tn or MM_TN, LANES)
    tk = _tile(k, tk or MM_TK, LANES)
    nk = k // tk
    in_specs = [pl.BlockSpec((tm, tk), lambda i, j, kk: (i, kk)),
                pl.BlockSpec((tk, tn), lambda i, j, kk: (kk, j))]
    args = [x, w]
    for arr, kind in extras:
        if kind == 'tile':
            in_specs.append(pl.BlockSpec((tm, tn), lambda i, j, kk: (i, j)))
        elif kind == 'row':
            in_specs.append(pl.BlockSpec((1, tn), lambda i, j, kk: (0, j)))
        elif kind == 'coltile':
            in_specs.append(pl.BlockSpec((LANES, tn), lambda i, j, kk: (0, j)))
        else:
            in_specs.append(pl.BlockSpec((tm, LANES), lambda i, j, kk: (i, 0)))
        args.append(arr)
    scratch = [pltpu.VMEM((tm, tn), F32)] if nk > 1 else []
    return pl.pallas_call(
        functools.partial(_mm_body, nk=nk, n_extra=len(extras), epilogue=epilogue),
        name=name,
        grid=(m // tm, n // tn, nk),
        in_specs=in_specs,
        out_specs=pl.BlockSpec((tm, tn), lambda i, j, kk: (i, j)),
        out_shape=jax.ShapeDtypeStruct((m, n), out_dtype),
        scratch_shapes=scratch,
        compiler_params=_cparams("parallel", "parallel", "arbitrary"),
    )(*args)


def _bmm_heads_body(x_ref, w_ref, o_ref):
    o_ref[...] = jnp.dot(x_ref[...], w_ref[0], preferred_element_type=F32).astype(o_ref.dtype)


def _bmm_heads(x, w, out_dtype=BF16):
    m = x.shape[0]
    h, din, dout = w.shape
    return pl.pallas_call(
        _bmm_heads_body,
        name="bmm_heads",
        grid=(h,),
        in_specs=[pl.BlockSpec((m, din), lambda i: (0, i)),
                  pl.BlockSpec((1, din, dout), lambda i: (i, 0, 0))],
        out_specs=pl.BlockSpec((m, dout), lambda i: (0, i)),
        out_shape=jax.ShapeDtypeStruct((m, h * dout), out_dtype),
        compiler_params=_cparams("parallel"),
    )(x, w)


def _ln_body(*refs, n_add, alpha):
    x_ref = refs[0]
    adds = refs[1:1 + n_add]
    g_ref, b_ref, o_ref, obf_ref = refs[1 + n_add:]
    h = alpha * x_ref[...]
    for a in adds:
        h = h + a[...]
    mu = jnp.mean(h, axis=-1, keepdims=True)
    c = h - mu
    var = jnp.mean(c * c, axis=-1, keepdims=True)
    y = c * lax.rsqrt(var + LN_EPS) * g_ref[...] + b_ref[...]
    o_ref[...] = y
    obf_ref[...] = y.astype(BF16)


def _ln_res(x, adds, g, b, alpha):
    n, d = x.shape
    tm = _tile(n, LN_TM, 16)
    row = pl.BlockSpec((tm, d), lambda i: (i, 0))
    vec = pl.BlockSpec((1, d), lambda i: (0, 0))
    return pl.pallas_call(
        functools.partial(_ln_body, n_add=len(adds), alpha=alpha),
        name="ln_res",
        grid=(n // tm,),
        in_specs=[row] * (1 + len(adds)) + [vec, vec],
        out_specs=[row, row],
        out_shape=[jax.ShapeDtypeStruct((n, d), F32), jax.ShapeDtypeStruct((n, d), BF16)],
        compiler_params=_cparams("parallel"),
    )(x, *adds, g.reshape(1, d), b.reshape(1, d))


def _gelu_tanh(y):
    return 0.5 * y * (1.0 + jnp.tanh(math.sqrt(2.0 / math.pi) * (y + 0.044715 * (y * y * y))))


def _s5_body(u_ref, bre_ref, bim_ref, cre_ref, cim_ref, lre_ref, lim_ref, d_ref, h0re_ref, h0im_ref,
             z_ref, zbf_ref, hre_ref, him_ref, car_re, car_im, *, t_rows, n_t):
    i = pl.program_id(2)

    @pl.when(i == 0)
    def _():
        car_re[...] = h0re_ref[0]
        car_im[...] = h0im_ref[0]

    u = u_ref[...]
    ub = u.astype(BF16)
    h_re = jnp.dot(ub, bre_ref[0], preferred_element_type=F32)
    h_im = jnp.dot(ub, bim_ref[0], preferred_element_type=F32)
    lr, li = lre_ref[0], lim_ref[0]
    cr, ci = car_re[...], car_im[...]
    row = lax.broadcasted_iota(jnp.int32, h_re.shape, 0)
    first = row == 0
    h_re = h_re + jnp.where(first, lr * cr - li * ci, 0.0)
    h_im = h_im + jnp.where(first, lr * ci + li * cr, 0.0)
    pr, pi = lr, li
    s = 1
    while s < t_rows:
        keep = row >= s
        sh_re = jnp.where(keep, pltpu.roll(h_re, s, 0), 0.0)
        sh_im = jnp.where(keep, pltpu.roll(h_im, s, 0), 0.0)
        h_re, h_im = h_re + pr * sh_re - pi * sh_im, h_im + pr * sh_im + pi * sh_re
        pr, pi = pr * pr - pi * pi, 2.0 * pr * pi
        s *= 2
    last_re = h_re[t_rows - 1:t_rows, :]
    last_im = h_im[t_rows - 1:t_rows, :]
    car_re[...] = last_re
    car_im[...] = last_im
    y = (jnp.dot(h_re.astype(BF16), cre_ref[0], preferred_element_type=F32)
         - jnp.dot(h_im.astype(BF16), cim_ref[0], preferred_element_type=F32))
    z = _gelu_tanh(y + d_ref[0] * u)
    z_ref[...] = z
    zbf_ref[...] = z.astype(BF16)

    @pl.when(i == n_t - 1)
    def _():
        hre_ref[0] = last_re
        him_ref[0] = last_im


def _s5_params(lam_re, lam_im, b_re, b_im, c_re, c_im, d, log_dt):
    g, p = lam_re.shape
    gs = b_re.shape[2]
    gpb = LANES // gs
    gb = g // gpb
    dt = jnp.exp(log_dt.astype(F32))[:, None]
    mag = jnp.exp(lam_re * dt)
    lb_re = mag * jnp.cos(lam_im * dt)
    lb_im = mag * jnp.sin(lam_im * dt)
    den = lam_re * lam_re + lam_im * lam_im
    q_re = ((lb_re - 1.0) * lam_re + lb_im * lam_im) / den
    q_im = (lb_im * lam_re - (lb_re - 1.0) * lam_im) / den
    bb_re = q_re[..., None] * b_re - q_im[..., None] * b_im
    bb_im = q_re[..., None] * b_im + q_im[..., None] * b_re
    eye = jnp.eye(gpb, dtype=F32)

    def pack_in(b):
        b = b.reshape(gb, gpb, p, gs).transpose(0, 1, 3, 2)
        return jnp.einsum('ab,xahp->xahbp', eye, b).reshape(gb, gpb * gs, gpb * p).astype(BF16)

    def pack_out(c):
        c = c.reshape(gb, gpb, gs, p).transpose(0, 1, 3, 2)
        return jnp.einsum('ab,xaph->xapbh', eye, c).reshape(gb, gpb * p, gpb * gs).astype(BF16)

    return dict(
        bre=pack_in(bb_re), bim=pack_in(bb_im), cre=pack_out(c_re.astype(F32)), cim=pack_out(c_im.astype(F32)),
        lre=lb_re.reshape(gb, 1, gpb * p), lim=lb_im.reshape(gb, 1, gpb * p),
        d=d.astype(F32).reshape(gb, 1, gpb * gs), gb=gb, s=gpb * p)


def _s5_scan(proj, row0, bsz, seq, prm, h0_re, h0_im):
    gb, s = prm['gb'], prm['s']
    t_rows = _tile(seq, S5_T, 8)
    n_t = seq // t_rows
    blk0 = row0 // t_rows
    assert row0 % t_rows == 0
    h0_re = h0_re.astype(F32).reshape(bsz * gb, 1, s)
    h0_im = h0_im.astype(F32).reshape(bsz * gb, 1, s)
    par = lambda shape: pl.BlockSpec((1,) + shape, lambda b, g, i: (g, 0, 0))
    st = pl.BlockSpec((1, 1, s), lambda b, g, i: (b * gb + g, 0, 0))
    zspec = pl.BlockSpec((t_rows, LANES), lambda b, g, i: (b * n_t + i, g))
    z, zbf, hre, him = pl.pallas_call(
        functools.partial(_s5_body, t_rows=t_rows, n_t=n_t),
        name="s5_scan",
        grid=(bsz, gb, n_t),
        in_specs=[pl.BlockSpec((t_rows, LANES), lambda b, g, i: (blk0 + b * n_t + i, g)),
                  par((LANES, s)), par((LANES, s)), par((s, LANES)), par((s, LANES)),
                  par((1, s)), par((1, s)), par((1, LANES)), st, st],
        out_specs=[zspec, zspec, st, st],
        out_shape=[jax.ShapeDtypeStruct((bsz * seq, gb * LANES), F32),
                   jax.ShapeDtypeStruct((bsz * seq, gb * LANES), BF16),
                   jax.ShapeDtypeStruct((bsz * gb, 1, s), F32),
                   jax.ShapeDtypeStruct((bsz * gb, 1, s), F32)],
        scratch_shapes=[pltpu.VMEM((1, s), F32), pltpu.VMEM((1, s), F32)],
        compiler_params=_cparams("parallel", "parallel", "arbitrary"),
    )(proj, prm['bre'], prm['bim'], prm['cre'], prm['cim'], prm['lre'], prm['lim'], prm['d'], h0_re, h0_im)
    return z, zbf, hre, him


def _softplus(x):
    return jnp.maximum(x, 0.0) + jnp.log1p(jnp.exp(-jnp.abs(x)))


def _rwkv_prep_body(pr_ref, pk_ref, pv_ref, pl_ref, sr_ref, sk_ref, sv_ref, sl_ref,
                    mr_ref, mk_ref, mv_ref, ml_ref, w0_ref, a0_ref, w2_ref, a2_ref, g2_ref,
                    r_ref, k_ref, v_ref, w_ref, a_ref, g_ref, cr, ck, cv, cl):
    i = pl.program_id(1)

    @pl.when(i == 0)
    def _():
        cr[...] = sr_ref[0]
        ck[...] = sk_ref[0]
        cv[...] = sv_ref[0]
        cl[...] = sl_ref[0]

    def mixed(p_ref, carry, mu_ref):
        x = p_ref[...]
        row = lax.broadcasted_iota(jnp.int32, x.shape, 0)
        prev = jnp.where(row == 0, carry[...], pltpu.roll(x, 1, 0))
        carry[...] = x[x.shape[0] - 1:, :]
        return x + (prev - x) * mu_ref[...]

    r_ref[...] = mixed(pr_ref, cr, mr_ref)
    k_ref[...] = mixed(pk_ref, ck, mk_ref)
    v_ref[...] = mixed(pv_ref, cv, mv_ref)
    lo = mixed(pl_ref, cl, ml_ref)
    tw = jnp.dot(jnp.tanh(lo).astype(BF16), w2_ref[...], preferred_element_type=F32)
    w_log = -_softplus(-(w0_ref[...] + tw)) - 0.5
    w_ref[...] = jnp.exp(-jnp.exp(w_log))
    a_ref[...] = jax.nn.sigmoid(a0_ref[...] + jnp.dot(lo.astype(BF16), a2_ref[...], preferred_element_type=F32))
    g_ref[...] = jnp.dot(jax.nn.sigmoid(lo).astype(BF16), g2_ref[...], preferred_element_type=F32)


def _rwkv_prep(proj, row0, bsz, seq, c, lp, col_r, shift, prm):
    t_rows = _tile(seq, RWKV_PREP_T, 8)
    n_t = seq // t_rows
    assert row0 % t_rows == 0 and col_r % c == 0 and (col_r + 3 * c) % lp == 0
    blk0 = row0 // t_rows
    cb = col_r // c
    lb = (col_r + 3 * c) // lp
    pin = lambda j: pl.BlockSpec((t_rows, c), lambda b, i: (blk0 + b * n_t + i, cb + j))
    sh = lambda w: pl.BlockSpec((1, 1, w), lambda b, i: (b, 0, 0))
    vec = lambda w: pl.BlockSpec((1, w), lambda b, i: (0, 0))
    mat = pl.BlockSpec((lp, c), lambda b, i: (0, 0))
    out = pl.BlockSpec((t_rows, c), lambda b, i: (b * n_t + i, 0))
    sds = jax.ShapeDtypeStruct((bsz * seq, c), F32)
    return pl.pallas_call(
        _rwkv_prep_body,
        name="rwkv_prep",
        grid=(bsz, n_t),
        in_specs=[pin(0), pin(1), pin(2),
                  pl.BlockSpec((t_rows, lp), lambda b, i: (blk0 + b * n_t + i, lb)),
                  sh(c), sh(c), sh(c), sh(lp), vec(c), vec(c), vec(c), vec(lp), vec(c), vec(c), mat, mat, mat],
        out_specs=[out] * 6,
        out_shape=[sds] * 6,
        scratch_shapes=[pltpu.VMEM((1, c), F32)] * 3 + [pltpu.VMEM((1, lp), F32)],
        compiler_params=_cparams("parallel", "arbitrary"),
    )(proj, proj, proj, proj, shift[0], shift[1], shift[2], shift[3],
      prm['mu_r'], prm['mu_k'], prm['mu_v'], prm['mu_l'], prm['w0'], prm['a0'], prm['w2'], prm['a2'], prm['g2'])


def _rwkv_scan_body(r_ref, k_ref, v_ref, w_ref, a_ref, kkp_ref, kap_ref, rkp_ref, lnw_ref, lnb_ref, s0_ref,
                    y_ref, sout_ref, st, kk_s, kka_s, km_s, wr_s, c1_s, c2_s, yr_s, *, t_rows, n_t, hd):
    i = pl.program_id(1)

    @pl.when(i == 0)
    def _():
        st[...] = s0_ref[...]

    r = r_ref[...]
    kraw = k_ref[...]
    a = a_ref[...]
    kk = kraw * kkp_ref[...][None]
    kk = kk * lax.rsqrt(jnp.maximum(jnp.sum(kk * kk, axis=1, keepdims=True), 1e-24))
    km = kraw * (1.0 + (a - 1.0) * kap_ref[...][None])
    kka = kk * a
    kk_s[...] = kk
    kka_s[...] = kka
    km_s[...] = km
    wr_s[...] = w_ref[...] * r
    c1_s[...] = jnp.sum(kka * r, axis=1, keepdims=True)
    c2_s[...] = jnp.sum(km * r, axis=1, keepdims=True)

    def step(t, carry):
        sk = jnp.zeros((hd, LANES), F32)
        yq = jnp.zeros((hd, LANES), F32)
        for j in range(hd):
            sj = st[j]
            sk = sk + sj * kk_s[t, pl.ds(j, 1), :]
            yq = yq + sj * wr_s[t, pl.ds(j, 1), :]
        v_t = v_ref[t]
        for j in range(hd):
            st[j] = (st[j] * w_ref[t, pl.ds(j, 1), :] - kka_s[t, pl.ds(j, 1), :] * sk
                     + km_s[t, pl.ds(j, 1), :] * v_t)
        yr_s[t] = yq - sk * c1_s[t] + v_t * c2_s[t]
        return carry

    lax.fori_loop(0, t_rows, step, 0)
    y = yr_s[...]
    ym = jnp.mean(y, axis=1, keepdims=True)
    yc = y - ym
    yv = jnp.mean(yc * yc, axis=1, keepdims=True)
    yn = yc * lax.rsqrt(yv + RWKV_GN_EPS) * lnw_ref[...][None] + lnb_ref[...][None]
    bonus = jnp.sum(r * km * rkp_ref[...][None], axis=1, keepdims=True) * v_ref[...]
    y_ref[...] = yn + bonus

    @pl.when(i == n_t - 1)
    def _():
        sout_ref[...] = st[...]


def _rwkv_scan(r, k, v, w, a, prm, s0):
    seq, hd, bh = r.shape
    t_rows = _tile(seq, RWKV_SCAN_T, 1)
    n_t = seq // t_rows
    n_lb = bh // LANES
    seq_spec = pl.BlockSpec((t_rows, hd, LANES), lambda l, i: (i, 0, l))
    par = pl.BlockSpec((hd, LANES), lambda l, i: (0, l))
    st_spec = pl.BlockSpec((hd, hd, LANES), lambda l, i: (0, 0, l))
    big = pltpu.VMEM((t_rows, hd, LANES), F32)
    small = pltpu.VMEM((t_rows, 1, LANES), F32)
    return pl.pallas_call(
        functools.partial(_rwkv_scan_body, t_rows=t_rows, n_t=n_t, hd=hd),
        name="rwkv_scan",
        grid=(n_lb, n_t),
        in_specs=[seq_spec] * 5 + [par] * 5 + [st_spec],
        out_specs=[seq_spec, st_spec],
        out_shape=[jax.ShapeDtypeStruct((seq, hd, bh), F32), jax.ShapeDtypeStruct((hd, hd, bh), F32)],
        scratch_shapes=[pltpu.VMEM((hd, hd, LANES), F32), big, big, big, big, small, small, big],
        compiler_params=_cparams("parallel", "arbitrary"),
    )(r, k, v, w, a, prm['k_k'], prm['k_a'], prm['r_k'], prm['ln_w'], prm['ln_b'], s0)


def _mul_cast_body(y_ref, g_ref, o_ref):
    o_ref[...] = (y_ref[...] * g_ref[...]).astype(o_ref.dtype)


def _mul_cast(y, g):
    n, d = y.shape
    tm = _tile(n, LN_TM * 2, 16)
    spec = pl.BlockSpec((tm, d), lambda i: (i, 0))
    return pl.pallas_call(
        _mul_cast_body, name="mul_cast", grid=(n // tm,), in_specs=[spec, spec], out_specs=spec,
        out_shape=jax.ShapeDtypeStruct((n, d), BF16), compiler_params=_cparams("parallel"),
    )(y, g)


def _rope_lanes(x, cos, sin, half):
    lane = lax.broadcasted_iota(jnp.int32, x.shape, 1)
    swapped = jnp.where(lane < half, pltpu.roll(x, LANES - half, 1), pltpu.roll(x, half, 1))
    return x * cos + swapped * sin


def _mla_prep_body(p_ref, qg_ref, kg_ref, cos_ref, sin_ref, cq_ref, cqt_ref, ckv_ref, ckr_ref, ckvt_ref, kr_ref,
                   *, ql, kl, half):
    x = p_ref[...]
    cq = x[:, :ql]
    cq = cq * lax.rsqrt(jnp.mean(cq * cq, axis=-1, keepdims=True) + RMS_EPS) * qg_ref[...]
    cq_ref[...] = cq.astype(BF16)
    cqt_ref[...] = cq.T.astype(BF16)
    ckv = x[:, ql:ql + kl]
    ckv = ckv * lax.rsqrt(jnp.mean(ckv * ckv, axis=-1, keepdims=True) + RMS_EPS) * kg_ref[...]
    ckv_ref[...] = ckv
    ckvt_ref[...] = ckv.T.astype(BF16)
    kr = _rope_lanes(x[:, ql + kl:], cos_ref[...], sin_ref[...], half)
    kr_ref[...] = kr
    ckr_ref[...] = jnp.concatenate([ckv, kr], axis=1).astype(BF16)


def _mla_prep(proj, q_norm, kv_norm, cos, sin, ql, kl, half):
    n, width = proj.shape
    tm = _tile(n, LN_TM, LANES)
    row = lambda w: pl.BlockSpec((tm, w), lambda i: (i, 0))
    col = lambda w: pl.BlockSpec((w, tm), lambda i: (0, i))
    vec = lambda w: pl.BlockSpec((1, w), lambda i: (0, 0))
    return pl.pallas_call(
        functools.partial(_mla_prep_body, ql=ql, kl=kl, half=half),
        name="mla_prep",
        grid=(n // tm,),
        in_specs=[row(width), vec(ql), vec(kl), row(LANES), row(LANES)],
        out_specs=[row(ql), col(ql), row(kl), row(kl + LANES), col(kl), row(LANES)],
        out_shape=[jax.ShapeDtypeStruct((n, ql), BF16), jax.ShapeDtypeStruct((ql, n), BF16),
                   jax.ShapeDtypeStruct((n, kl), F32), jax.ShapeDtypeStruct((n, kl + LANES), BF16),
                   jax.ShapeDtypeStruct((kl, n), BF16), jax.ShapeDtypeStruct((n, LANES), F32)],
        compiler_params=_cparams("parallel"),
    )(proj, q_norm.reshape(1, ql), kv_norm.reshape(1, kl), cos, sin)


def _online_softmax_update(s, v_blk, m_s, l_s, acc_s):
    m_prev = m_s[...]
    m_new = jnp.maximum(m_prev, jnp.max(s, axis=1, keepdims=True))
    alpha = jnp.exp(m_prev - m_new)
    p = jnp.exp(s - m_new)
    l_s[...] = alpha * l_s[...] + jnp.sum(p, axis=1, keepdims=True)
    acc_s[...] = alpha * acc_s[...] + jnp.dot(p.astype(BF16), v_blk, preferred_element_type=F32)
    m_s[...] = m_new


_NT = (((1,), (1,)), ((), ()))


def _attn_prompt_body(qt_ref, k_ref, vt_ref, o_ref, m_s, l_s, acc_s, *, tq, c):
    i = pl.program_id(2)
    m_s[...] = jnp.full(m_s.shape, MASK_VALUE, F32)
    l_s[...] = jnp.zeros(l_s.shape, F32)
    acc_s[...] = jnp.zeros(acc_s.shape, F32)
    qt = qt_ref[...]

    def update(j, masked):
        off = pl.multiple_of(j * tq, tq)
        s = jnp.dot(k_ref[pl.ds(off, tq), :], qt, preferred_element_type=F32)
        if masked:
            key = lax.broadcasted_iota(jnp.int32, s.shape, 0)
            qry = lax.broadcasted_iota(jnp.int32, s.shape, 1)
            s = jnp.where(key // CHUNK <= qry // CHUNK, s, MASK_VALUE)
        m_prev = m_s[...]
        m_new = jnp.maximum(m_prev, jnp.max(s, axis=0, keepdims=True))
        alpha = jnp.exp2((m_prev - m_new) * c)
        p = jnp.exp2((s - m_new) * c)
        l_s[...] = alpha * l_s[...] + jnp.sum(p, axis=0, keepdims=True)
        acc_s[...] = alpha * acc_s[...] + jnp.dot(vt_ref[:, pl.ds(off, tq)], p.astype(BF16),
                                                  preferred_element_type=F32)
        m_s[...] = m_new

    def full_tile(j, carry):
        update(j, False)
        return carry

    lax.fori_loop(0, i, full_tile, 0)
    update(i, True)
    o_ref[...] = (acc_s[...] / l_s[...]).T.astype(o_ref.dtype)


def _attn_prompt(qt, k, vt, bsz, seq, heads, scale):
    dk = qt.shape[0] // heads
    tq = _tile(seq, ATTN_TQ, LANES)
    assert tq % CHUNK == 0
    n_q = seq // tq
    return pl.pallas_call(
        functools.partial(_attn_prompt_body, tq=tq, c=scale * math.log2(math.e)),
        name="attn_prompt",
        grid=(bsz, heads, n_q),
        in_specs=[pl.BlockSpec((dk, tq), lambda b, h, i: (h, b * n_q + i)),
                  pl.BlockSpec((seq, dk), lambda b, h, i: (b, h)),
                  pl.BlockSpec((LANES, seq), lambda b, h, i: (h, b))],
        out_specs=pl.BlockSpec((tq, LANES), lambda b, h, i: (b * n_q + i, h)),
        out_shape=jax.ShapeDtypeStruct((bsz * seq, heads * LANES), BF16),
        scratch_shapes=[pltpu.VMEM((1, tq), F32), pltpu.VMEM((1, tq), F32), pltpu.VMEM((LANES, tq), F32)],
        compiler_params=_cparams("parallel", "parallel", "arbitrary"),
    )(qt, k, vt)


def _attn_latent_body(ql_ref, qr_ref, kv_ref, kr_ref, o_ref, m_s, l_s, acc_s,
                      *, tq, tk, n_k, heads, q_pos0, n_keys, scale):
    i = pl.program_id(1)
    m_s[...] = jnp.full(m_s.shape, MASK_VALUE, F32)
    l_s[...] = jnp.zeros(l_s.shape, F32)
    acc_s[...] = jnp.zeros(acc_s.shape, F32)
    ql = ql_ref[...]
    qr = qr_ref[...]

    def tile(j, carry):
        off = pl.multiple_of(j * tk, tk)
        kv = kv_ref[pl.ds(off, tk), :]
        s = lax.dot_general(ql, kv, _NT, preferred_element_type=F32)
        s = s + lax.dot_general(qr, kr_ref[pl.ds(off, tk), :], _NT, preferred_element_type=F32)
        row = lax.broadcasted_iota(jnp.int32, s.shape, 0)
        col = lax.broadcasted_iota(jnp.int32, s.shape, 1)
        q_pos = q_pos0 + (i * tq + row) // heads
        k_pos = off + col
        visible = (k_pos // CHUNK <= q_pos // CHUNK) & (k_pos < n_keys)
        s = jnp.where(visible, s * scale, MASK_VALUE)
        _online_softmax_update(s, kv, m_s, l_s, acc_s)
        return carry

    lax.fori_loop(0, n_k, tile, 0)
    o_ref[...] = (acc_s[...] / l_s[...]).astype(o_ref.dtype)


def _attn_latent(q_lat, q_rope, kv_all, kr_all, bsz, rows, heads, q_pos0, n_keys, scale):
    kl = q_lat.shape[1]
    nkp = kv_all.shape[0] // bsz
    tq = _tile(rows, ATTN_TQ, 16)
    tk = _tile(nkp, ATTN_TQ, 16)
    n_q = rows // tq
    return pl.pallas_call(
        functools.partial(_attn_latent_body, tq=tq, tk=tk, n_k=nkp // tk, heads=heads,
                          q_pos0=q_pos0, n_keys=n_keys, scale=scale),
        name="attn_latent",
        grid=(bsz, n_q),
        in_specs=[pl.BlockSpec((tq, kl), lambda b, i: (b * n_q + i, 0)),
                  pl.BlockSpec((tq, LANES), lambda b, i: (b * n_q + i, 0)),
                  pl.BlockSpec((nkp, kl), lambda b, i: (b, 0)),
                  pl.BlockSpec((nkp, LANES), lambda b, i: (b, 0))],
        out_specs=pl.BlockSpec((tq, kl), lambda b, i: (b * n_q + i, 0)),
        out_shape=jax.ShapeDtypeStruct((bsz * rows, kl), BF16),
        scratch_shapes=[pltpu.VMEM((tq, 1), F32), pltpu.VMEM((tq, 1), F32), pltpu.VMEM((tq, kl), F32)],
        compiler_params=_cparams("parallel", "arbitrary"),
    )(q_lat, q_rope, kv_all, kr_all)


def _router_body(x_ref, wh_ref, wl_ref, o_ref):
    x = x_ref[...]
    xh = x.astype(BF16)
    xl = (x - xh.astype(F32)).astype(BF16)
    wh = wh_ref[...]
    acc = jnp.dot(xh, wh, preferred_element_type=F32)
    acc = acc + jnp.dot(xl, wh, preferred_element_type=F32)
    acc = acc + jnp.dot(xh, wl_ref[...], preferred_element_type=F32)
    o_ref[...] = jax.nn.sigmoid(acc)


def _router(x, w):
    n, d = x.shape
    e = w.shape[1]
    tm = _tile(n, ROUTER_TM, 8)
    wh = w.astype(BF16)
    wl = (w - wh.astype(F32)).astype(BF16)
    wspec = pl.BlockSpec((d, e), lambda i: (0, 0))
    return pl.pallas_call(
        _router_body, name="moe_router", grid=(n // tm,),
        in_specs=[pl.BlockSpec((tm, d), lambda i: (i, 0)), wspec, wspec],
        out_specs=pl.BlockSpec((tm, e), lambda i: (i, 0)),
        out_shape=jax.ShapeDtypeStruct((n, e), F32),
        compiler_params=_cparams("parallel"),
    )(x, wh, wl)


def _rank_body(sel_ref, cum_ref, cnt_ref, carry):
    @pl.when(pl.program_id(0) == 0)
    def _():
        carry[...] = jnp.zeros(carry.shape, F32)

    sel = sel_ref[...]
    t = sel.shape[0]
    earlier = (lax.broadcasted_iota(jnp.int32, (t, t), 1) < lax.broadcasted_iota(jnp.int32, (t, t), 0))
    cum = jnp.dot(earlier.astype(BF16), sel, preferred_element_type=F32) + carry[...]
    cum_ref[...] = cum.astype(jnp.int32)
    carry[...] = carry[...] + jnp.sum(sel.astype(F32), axis=0, keepdims=True)
    cnt_ref[...] = carry[...].astype(jnp.int32)


def _rank(sel):
    n, e = sel.shape
    t = _tile(n, ROUTER_TM, 16)
    return pl.pallas_call(
        _rank_body, name="moe_rank", grid=(n // t,),
        in_specs=[pl.BlockSpec((t, e), lambda i: (i, 0))],
        out_specs=[pl.BlockSpec((t, e), lambda i: (i, 0)), pl.BlockSpec((1, e), lambda i: (0, 0))],
        out_shape=[jax.ShapeDtypeStruct((n, e), jnp.int32), jax.ShapeDtypeStruct((1, e), jnp.int32)],
        scratch_shapes=[pltpu.VMEM((1, e), F32)],
        compiler_params=_cparams("arbitrary"),
    )(sel)


def _moe_block(b, be_ref, nu_ref):
    bb = jnp.minimum(b, nu_ref[0] - 1)
    changed = (b == 0) | (be_ref[bb] != be_ref[jnp.maximum(bb - 1, 0)])
    return b < nu_ref[0], changed


def _moe_up_body(be_ref, nu_ref, x_ref, wg_ref, wu_ref, o_ref, wgb, wub):
    active, changed = _moe_block(pl.program_id(1), be_ref, nu_ref)

    @pl.when(active & changed)
    def _():
        wgb[...] = wg_ref[0, 0].astype(BF16)
        wub[...] = wu_ref[0, 0].astype(BF16)

    @pl.when(active)
    def _():
        x = x_ref[...]
        h1 = jnp.dot(x, wgb[...], preferred_element_type=F32)
        h2 = jnp.dot(x, wub[...], preferred_element_type=F32)
        o_ref[...] = (h1 * jax.nn.sigmoid(h1) * h2).astype(o_ref.dtype)


def _moe_down_body(be_ref, nu_ref, h_ref, wd_ref, o_ref, wdb):
    active, changed = _moe_block(pl.program_id(1), be_ref, nu_ref)

    @pl.when(active & changed)
    def _():
        wdb[...] = wd_ref[0, 0].astype(BF16)

    @pl.when(active)
    def _():
        o_ref[...] = jnp.dot(h_ref[...], wdb[...], preferred_element_type=F32)


def _expert_ffn(xs, blk_expert, n_used, w_gate, w_up, w_down, layer, tm):
    p, d = xs.shape
    ff = w_gate.shape[3]
    n_blk = p // tm
    tf = _tile(ff, MOE_TF, LANES)
    tn = _tile(d, MOE_TN, LANES)
    blk = lambda b, nu: jnp.minimum(b, nu[0] - 1)
    hid = pl.pallas_call(
        _moe_up_body, name="moe_up",
        grid_spec=pltpu.PrefetchScalarGridSpec(
            num_scalar_prefetch=2, grid=(ff // tf, n_blk),
            in_specs=[pl.BlockSpec((tm, d), lambda f, b, be, nu: (blk(b, nu), 0)),
                      pl.BlockSpec((1, 1, d, tf), lambda f, b, be, nu: (layer, be[blk(b, nu)], 0, f)),
                      pl.BlockSpec((1, 1, d, tf), lambda f, b, be, nu: (layer, be[blk(b, nu)], 0, f))],
            out_specs=pl.BlockSpec((tm, tf), lambda f, b, be, nu: (blk(b, nu), f)),
            scratch_shapes=[pltpu.VMEM((d, tf), BF16), pltpu.VMEM((d, tf), BF16)]),
        out_shape=jax.ShapeDtypeStruct((p, ff), BF16),
        compiler_params=_cparams("arbitrary", "arbitrary"),
    )(blk_expert, n_used, xs, w_gate, w_up)
    return pl.pallas_call(
        _moe_down_body, name="moe_down",
        grid_spec=pltpu.PrefetchScalarGridSpec(
            num_scalar_prefetch=2, grid=(d // tn, n_blk),
            in_specs=[pl.BlockSpec((tm, ff), lambda n, b, be, nu: (blk(b, nu), 0)),
                      pl.BlockSpec((1, 1, ff, tn), lambda n, b, be, nu: (layer, be[blk(b, nu)], 0, n))],
            out_specs=pl.BlockSpec((tm, tn), lambda n, b, be, nu: (blk(b, nu), n)),
            scratch_shapes=[pltpu.VMEM((ff, tn), BF16)]),
        out_shape=jax.ShapeDtypeStruct((p, d), F32),
        compiler_params=_cparams("arbitrary", "arbitrary"),
    )(blk_expert, n_used, hid, w_down)


def _moe_ffn(x, x_bf, layer, w_router, b_router, w_gate, w_up, w_down, ws_gate, ws_up, ws_down):
    n_tok, d = x.shape
    n_exp = w_router.shape[2]
    scores = _router(x, w_router[layer].astype(F32))
    biased = scores + b_router[layer].astype(F32)
    grp_score = lax.top_k(biased.reshape(n_tok, N_GROUP, n_exp // N_GROUP), 2)[0].sum(-1)
    _, grp_idx = lax.top_k(grp_score, TOPK_GROUP)
    grp_keep = jnp.any(grp_idx[:, :, None] == jnp.arange(N_GROUP)[None, None, :], axis=1)
    keep = jnp.repeat(grp_keep, n_exp // N_GROUP, axis=-1)
    _, top_idx = lax.top_k(jnp.where(keep, biased, -jnp.inf), TOP_K)
    top_w = jnp.take_along_axis(scores, top_idx, axis=-1)
    top_w = top_w / (top_w.sum(-1, keepdims=True) + 1e-20) * ROUTED_SCALE
    n_asg = n_tok * TOP_K
    tm = _tile(n_tok, MOE_TM, 16)
    n_blk = -(-n_asg // tm) + n_exp
    sel = jnp.any(top_idx[:, :, None] == jnp.arange(n_exp, dtype=top_idx.dtype)[None, None, :], axis=1)
    before, counts = _rank(sel.astype(BF16))
    counts = counts[0]
    padded = (counts + tm - 1) // tm * tm
    pad_end = jnp.cumsum(padded)
    pad_start = pad_end - padded
    dest = (pad_start[top_idx] + jnp.take_along_axis(before, top_idx, axis=-1)).astype(jnp.int32)
    tok = jnp.broadcast_to(jnp.arange(n_tok, dtype=jnp.int32)[:, None], dest.shape)
    slot_tok = jnp.full((n_blk * tm,), n_tok, jnp.int32).at[dest.reshape(-1)].set(tok.reshape(-1))
    blk_expert = jnp.minimum(jnp.searchsorted(pad_end, jnp.arange(n_blk) * tm, side='right'),
                             n_exp - 1).astype(jnp.int32)
    n_used = (pad_end[-1] // tm).astype(jnp.int32).reshape(1)
    x_pad = jnp.concatenate([x_bf, jnp.zeros((1, d), BF16)], axis=0)
    xs = x_pad[slot_tok]
    y_slots = _expert_ffn(xs, blk_expert, n_used, w_gate, w_up, w_down, layer, tm)
    routed = (y_slots[dest] * top_w[:, :, None]).sum(axis=1)
    n_sblk = n_tok // tm
    shared = _expert_ffn(x_bf, jnp.zeros((n_sblk,), jnp.int32), jnp.full((1,), n_sblk, jnp.int32),
                         ws_gate[:, None], ws_up[:, None], ws_down[:, None], layer, tm)
    return routed, shared


def _to_scan_layout(t, bsz, seq, heads, hd, bh_pad):
    t = t.reshape(bsz, seq, heads, hd).transpose(1, 3, 0, 2).reshape(seq, hd, bsz * heads)
    return jnp.pad(t, ((0, 0), (0, 0), (0, bh_pad - bsz * heads)))


def _rope_tables(pos, half):
    inv = jnp.exp(-math.log(ROPE_THETA) * jnp.arange(half, dtype=F32) / half)
    ang = pos.astype(F32)[:, None] * inv[None, :]
    cos, sin = jnp.cos(ang), jnp.sin(ang)
    pad = LANES - 2 * half
    ones = jnp.ones((pos.shape[0], pad), F32)
    zeros = jnp.zeros((pos.shape[0], pad), F32)
    return jnp.concatenate([cos, cos, ones], 1), jnp.concatenate([-sin, sin, zeros], 1)


def kernel(x_prompt, x_sample, state_s5_re, state_s5_im, state_rwkv, state_rwkv_shift, cache_c_kv, cache_k_rope, l0_w_in, s5_lambda_re, s5_lambda_im, s5_b_re, s5_b_im, s5_c_re, s5_c_im, s5_d, s5_log_dt, s5_w_glu, s5_b_glu, rwkv_mu, rwkv_w0, rwkv_w2, rwkv_a0, rwkv_a2, rwkv_g2, rwkv_k_k, rwkv_k_a, rwkv_r_k, rwkv_ln_w, rwkv_ln_b, l0_w_out, l1_w_in, mla_q_norm, mla_w_uq, mla_kv_norm, mla_w_uk, mla_w_uv, l1_w_out, ln_mix_g, ln_mix_b, ln_ffn_g, ln_ffn_b, moe_w_router, moe_b_router, moe_w_gate, moe_w_up, moe_w_down, moe_ws_gate, moe_ws_up, moe_ws_down):
    bp, lp_, d = x_prompt.shape
    bs, ls, _ = x_sample.shape
    n_p, n_s = bp * lp_, bs * ls
    depth = ln_mix_g.shape[0]
    alpha = (2 * depth) ** 0.25
    groups = (("p", bp, lp_, 0), ("s", bs, ls, n_p))

    x = jnp.concatenate([x_prompt.reshape(n_p, d), x_sample.reshape(n_s, d)], axis=0).astype(F32)
    x_bf = x.astype(BF16)
    moe_p = (moe_w_router, moe_b_router, moe_w_gate, moe_w_up, moe_w_down, moe_ws_gate, moe_ws_up, moe_ws_down)

    s5_g, s5_p = s5_lambda_re.shape
    ws = s5_g * s5_b_re.shape[2]
    heads_r, hd = rwkv_r_k.shape
    c = heads_r * hd
    lora_w, lora_a, lora_g = rwkv_w2.shape[0], rwkv_a2.shape[0], rwkv_g2.shape[0]
    lora = lora_w + lora_a + lora_g
    lp = -(-lora // LANES) * LANES
    proj_w = ws + 3 * c + lora
    w_in = jnp.pad(l0_w_in, ((0, 0), (0, lp - lora))).astype(BF16)
    proj = _mm(x_bf, w_in, tn=512, name="mm_l0_in")
    s5p = _s5_params(s5_lambda_re.astype(F32), s5_lambda_im.astype(F32), s5_b_re.astype(F32),
                     s5_b_im.astype(F32), s5_c_re, s5_c_im, s5_d, s5_log_dt)
    mu = jnp.pad(rwkv_mu.astype(F32), (0, lp - lora)).reshape(1, -1)
    row_pad = lambda m, at: jnp.pad(m.astype(F32), ((at, lp - at - m.shape[0]), (0, 0))).astype(BF16)
    prep_p = dict(mu_r=mu[:, :c], mu_k=mu[:, c:2 * c], mu_v=mu[:, 2 * c:3 * c], mu_l=mu[:, 3 * c:],
                  w0=rwkv_w0.astype(F32).reshape(1, c), a0=rwkv_a0.astype(F32).reshape(1, c),
                  w2=row_pad(rwkv_w2, 0), a2=row_pad(rwkv_a2, lora_w), g2=row_pad(rwkv_g2, lora_w + lora_a))

    zs, zbfs, yrs, outs0 = [], [], [], {}
    for name, bsz, seq, row0 in groups:
        if name == "p":
            h0re = jnp.zeros((bsz, s5_g, s5_p), F32)
            h0im = h0re
            s0 = jnp.zeros((bsz, heads_r, hd, hd), F32)
            shift0 = jnp.zeros((bsz, proj_w - ws), F32)
        else:
            h0re, h0im, s0, shift0 = state_s5_re, state_s5_im, state_rwkv.astype(F32), state_rwkv_shift.astype(F32)
        z, zbf, hre, him = _s5_scan(proj, row0, bsz, seq, s5p, h0re, h0im)
        zs.append(z)
        zbfs.append(zbf)
        shift0 = jnp.pad(shift0, ((0, 0), (0, lp - lora)))[:, None, :]
        shifts = (shift0[..., :c], shift0[..., c:2 * c], shift0[..., 2 * c:3 * c], shift0[..., 3 * c:])
        r, k, v, w, a, g = _rwkv_prep(proj, row0, bsz, seq, c, lp, ws, shifts, prep_p)
        bh = bsz * heads_r
        bh_pad = -(-bh // LANES) * LANES
        lay = lambda t: _to_scan_layout(t, bsz, seq, heads_r, hd, bh_pad)
        par = lambda t: jnp.pad(jnp.tile(t.astype(F32).reshape(heads_r, hd).T, (1, bsz)), ((0, 0), (0, bh_pad - bh)))
        scan_p = dict(k_k=par(rwkv_k_k), k_a=par(rwkv_k_a), r_k=par(rwkv_r_k), ln_w=par(rwkv_ln_w), ln_b=par(rwkv_ln_b))
        s0l = jnp.pad(s0.transpose(3, 2, 0, 1).reshape(hd, hd, bh), ((0, 0), (0, 0), (0, bh_pad - bh)))
        y_scan, s_last = _rwkv_scan(lay(r), lay(k), lay(v), lay(w), lay(a), scan_p, s0l)
        y_nat = y_scan[:, :, :bh].reshape(seq, hd, bsz, heads_r).transpose(2, 0, 3, 1).reshape(bsz * seq, c)
        yrs.append(_mul_cast(y_nat, g))
        last = proj[row0:row0 + bsz * seq].reshape(bsz, seq, -1)[:, -1, ws:proj_w]
        outs0[name] = (hre.reshape(bsz, s5_g, s5_p).astype(state_s5_re.dtype),
                       him.reshape(bsz, s5_g, s5_p).astype(state_s5_im.dtype),
                       s_last[:, :, :bh].reshape(hd, hd, bsz, heads_r).transpose(2, 3, 1, 0).astype(state_rwkv.dtype),
                       last.astype(state_rwkv_shift.dtype))
    z = jnp.concatenate(zs, axis=0)
    zbf = jnp.concatenate(zbfs, axis=0)
    y_a = _mm(zbf, s5_w_glu.astype(BF16), out_dtype=BF16, name="mm_glu",
              epilogue=lambda acc, zt, b: zt * jax.nn.sigmoid(acc + b),
              extras=((z, 'tile'), (s5_b_glu.astype(F32).reshape(1, ws), 'row')))
    mixed = jnp.concatenate([y_a, jnp.concatenate(yrs, axis=0)], axis=1)
    mix = _mm(mixed, l0_w_out.astype(BF16), name="mm_l0_out")
    x, x_bf = _ln_res(x, [mix], ln_mix_g[0].astype(F32), ln_mix_b[0].astype(F32), alpha)
    routed, shared = _moe_ffn(x, x_bf, 0, *moe_p)
    x, x_bf = _ln_res(x, [routed, shared], ln_ffn_g[0].astype(F32), ln_ffn_b[0].astype(F32), alpha)

    ql, heads_a, qd = mla_w_uq.shape
    kl, _, nope = mla_w_uk.shape
    vd = mla_w_uv.shape[2]
    rope = qd - nope
    half = rope // 2
    past = cache_c_kv.shape[1]
    assert nope == LANES and vd == LANES and rope <= LANES
    scale = float(qd) ** -0.5
    w1 = jnp.pad(l1_w_in, ((0, 0), (0, LANES - rope))).astype(BF16)
    proj1 = _mm(x_bf, w1, tm=512, tn=ql + kl + LANES, name="mm_l1_in")
    pos = jnp.concatenate([jnp.tile(jnp.arange(lp_, dtype=jnp.int32), bp),
                           jnp.tile(past + jnp.arange(ls, dtype=jnp.int32), bs)])
    cos, sin = _rope_tables(pos, half)
    cqn, cqn_t, ckv, ckr, ckv_t, kr = _mla_prep(proj1, mla_q_norm.astype(F32), mla_kv_norm.astype(F32),
                                                cos, sin, ql, kl, half)
    w_qn3 = mla_w_uq[:, :, :nope]
    w_qr3 = jnp.pad(mla_w_uq[:, :, nope:], ((0, 0), (0, 0), (0, LANES - rope)))
    w_qt = jnp.stack([w_qn3, w_qr3], axis=2).reshape(ql, heads_a * 2 * LANES).T.astype(BF16)

    def rope_rows_epilogue(acc, cos_t, sin_t):
        out = []
        for j in range(acc.shape[0] // LANES):
            blk = acc[j * LANES:(j + 1) * LANES, :]
            if j % 2 == 1:
                row = lax.broadcasted_iota(jnp.int32, blk.shape, 0)
                swapped = jnp.where(row < half, pltpu.roll(blk, LANES - half, 0), pltpu.roll(blk, half, 0))
                blk = blk * cos_t + swapped * sin_t
            out.append(blk)
        return jnp.concatenate(out, axis=0)

    q_t = _mm(w_qt, cqn_t, n=n_p, out_dtype=BF16, epilogue=rope_rows_epilogue, name="mm_qt",
              extras=((cos[:n_p].T, 'coltile'), (sin[:n_p].T, 'coltile')))
    eye_rows = jnp.broadcast_to(jnp.eye(LANES, dtype=F32)[:, None, :], (LANES, heads_a, LANES))
    w_k = jnp.concatenate([jnp.stack([mla_w_uk.astype(F32), jnp.zeros((kl, heads_a, LANES), F32)], axis=2),
                           jnp.stack([jnp.zeros((LANES, heads_a, nope), F32), eye_rows], axis=2)], axis=0)
    k_cat = _mm(ckr, w_k.reshape(kl + LANES, heads_a * 2 * LANES).astype(BF16), m=n_p, out_dtype=BF16, name="mm_kcat")
    v_t = _mm(mla_w_uv.reshape(kl, heads_a * vd).T.astype(BF16), ckv_t, n=n_p, out_dtype=BF16, name="mm_vt")
    o_p = _attn_prompt(q_t, k_cat, v_t, bp, lp_, heads_a, scale)
    cqn_s = cqn[n_p:]

    def rope_epilogue(acc, cos_t, sin_t):
        return jnp.concatenate([_rope_lanes(acc[:, j * LANES:(j + 1) * LANES], cos_t, sin_t, half)
                                for j in range(acc.shape[1] // LANES)], axis=1)

    q_nope_s = _mm(cqn_s, w_qn3.reshape(ql, heads_a * nope).astype(BF16), out_dtype=BF16)
    q_rope_s = _mm(cqn_s, w_qr3.reshape(ql, heads_a * LANES).astype(BF16), out_dtype=BF16, epilogue=rope_epilogue,
                   extras=((cos[n_p:], 'rowtile'), (sin[n_p:], 'rowtile')))
    q_lat = _bmm_heads(q_nope_s, mla_w_uk.transpose(1, 2, 0).astype(BF16))
    n_keys = past + ls
    nkp = -(-n_keys // ATTN_TQ) * ATTN_TQ
    kv_all = jnp.concatenate([cache_c_kv.astype(F32), ckv[n_p:].reshape(bs, ls, kl)], axis=1)
    kr_all = jnp.concatenate([jnp.pad(cache_k_rope.astype(F32), ((0, 0), (0, 0), (0, LANES - rope))),
                              kr[n_p:].reshape(bs, ls, LANES)], axis=1)
    padk = lambda t: jnp.pad(t, ((0, 0), (0, nkp - n_keys), (0, 0))).astype(BF16).reshape(bs * nkp, -1)
    o_lat = _attn_latent(q_lat.reshape(n_s * heads_a, kl), q_rope_s.reshape(n_s * heads_a, LANES),
                         padk(kv_all), padk(kr_all), bs, ls * heads_a, heads_a, past, n_keys, scale)
    o_s = _bmm_heads(o_lat.reshape(n_s, heads_a * kl), mla_w_uv.transpose(1, 0, 2).astype(BF16))
    mix = _mm(jnp.concatenate([o_p, o_s], axis=0), l1_w_out.astype(BF16), name="mm_l1_out")
    x, x_bf = _ln_res(x, [mix], ln_mix_g[1].astype(F32), ln_mix_b[1].astype(F32), alpha)
    routed, shared = _moe_ffn(x, x_bf, 1, *moe_p)
    x, _ = _ln_res(x, [routed, shared], ln_ffn_g[1].astype(F32), ln_ffn_b[1].astype(F32), alpha)

    dt = x_prompt.dtype
    y_prompt = x[:n_p].reshape(bp, lp_, d).astype(dt)
    y_sample = x[n_p:].reshape(bs, ls, d).astype(dt)
    ckv_p = ckv[:n_p].reshape(bp, lp_, kl).astype(dt)
    ckv_s = ckv[n_p:].reshape(bs, ls, kl).astype(dt)
    kr_p = kr[:n_p, :rope].reshape(bp, lp_, rope).astype(dt)
    kr_s = kr[n_p:, :rope].reshape(bs, ls, rope).astype(dt)
    return (y_prompt, y_sample, *outs0["p"], ckv_p, kr_p, *outs0["s"], ckv_s, kr_s)
```

```python
import functools
import math

import jax
import jax.numpy as jnp
from jax import lax
from jax.experimental import pallas as pl
from jax.experimental.pallas import tpu as pltpu

F32 = jnp.float32
BF16 = jnp.bfloat16

CHUNK = 64
LN_EPS = 1e-5
RMS_EPS = 1e-6
RWKV_GN_EPS = 64e-5
ROPE_THETA = 10000.0
TOP_K = 8
N_GROUP = 8
TOPK_GROUP = 4
ROUTED_SCALE = 2.5
MASK_VALUE = -1e30

LANES = 128
V7X_VMEM_BYTES = 64 * 2**20
VMEM_LIMIT = V7X_VMEM_BYTES - 8 * 2**20

MM_TM = 1024
MM_TN = 1024
MM_TK = 4096
LN_TM = 256
S5_T = 512
RWKV_PREP_T = 256
RWKV_SCAN_T = 64
ATTN_TQ = 1024
LATENT_T = 512
MOE_TM = 256
MOE_TF = 512
MOE_TN = 2048
ROUTER_TM = 512


def _tile(n, target, align):
    if n <= target:
        return n
    t = (target // align) * align
    while t >= align:
        if n % t == 0:
            return t
        t -= align
    return n


def _cparams(*sems):
    return pltpu.CompilerParams(dimension_semantics=sems, vmem_limit_bytes=VMEM_LIMIT)


def _mm_body(*refs, nk, n_extra, epilogue):
    x_ref, w_ref = refs[0], refs[1]
    extras = refs[2:2 + n_extra]
    o_ref = refs[2 + n_extra]
    part = jnp.dot(x_ref[...], w_ref[...], preferred_element_type=F32)

    def finish(acc):
        if epilogue is not None:
            acc = epilogue(acc, *[e[...] for e in extras])
        o_ref[...] = acc.astype(o_ref.dtype)

    if nk == 1:
        finish(part)
        return
    acc_ref = refs[3 + n_extra]
    k = pl.program_id(2)

    @pl.when(k == 0)
    def _():
        acc_ref[...] = part

    @pl.when(k > 0)
    def _():
        acc_ref[...] += part

    @pl.when(k == nk - 1)
    def _():
        finish(acc_ref[...])


def _mm(x, w, *, out_dtype=F32, tm=None, tn=None, tk=None, epilogue=None, extras=(), m=None, n=None, name="mm"):
    k = x.shape[1]
    m = m or x.shape[0]
    n = n or w.shape[1]
    tm = _tile(m, tm or MM_TM, 16)
    tn = _tile(n, tn or MM_TN, LANES)
    tk = _tile(k, tk or MM_TK, LANES)
    nk = k // tk
    in_specs = [pl.BlockSpec((tm, tk), lambda i, j, kk: (i, kk)),
                pl.BlockSpec((tk, tn), lambda i, j, kk: (kk, j))]
    args = [x, w]
    for arr, kind in extras:
        if kind == 'tile':
            in_specs.append(pl.BlockSpec((tm, tn), lambda i, j, kk: (i, j)))
        elif kind == 'row':
            in_specs.append(pl.BlockSpec((1, tn), lambda i, j, kk: (0, j)))
        elif kind == 'coltile':
            in_specs.append(pl.BlockSpec((LANES, tn), lambda i, j, kk: (0, j)))
        else:
            in_specs.append(pl.BlockSpec((tm, LANES), lambda i, j, kk: (i, 0)))
        args.append(arr)
    scratch = [pltpu.VMEM((tm, tn), F32)] if nk > 1 else []
    return pl.pallas_call(
        functools.partial(_mm_body, nk=nk, n_extra=len(extras), epilogue=epilogue),
        name=name,
        grid=(m // tm, n // tn, nk),
        in_specs=in_specs,
        out_specs=pl.BlockSpec((tm, tn), lambda i, j, kk: (i, j)),
        out_shape=jax.ShapeDtypeStruct((m, n), out_dtype),
        scratch_shapes=scratch,
        compiler_params=_cparams("parallel", "parallel", "arbitrary"),
    )(*args)


def _bmm_heads_body(x_ref, w_ref, o_ref):
    o_ref[...] = jnp.dot(x_ref[...], w_ref[0], preferred_element_type=F32).astype(o_ref.dtype)


def _bmm_heads(x, w, out_dtype=BF16):
    m = x.shape[0]
    h, din, dout = w.shape
    return pl.pallas_call(
        _bmm_heads_body,
        name="bmm_heads",
        grid=(h,),
        in_specs=[pl.BlockSpec((m, din), lambda i: (0, i)),
                  pl.BlockSpec((1, din, dout), lambda i: (i, 0, 0))],
        out_specs=pl.BlockSpec((m, dout), lambda i: (0, i)),
        out_shape=jax.ShapeDtypeStruct((m, h * dout), out_dtype),
        compiler_params=_cparams("parallel"),
    )(x, w)


def _ln_body(*refs, n_add, alpha):
    x_ref = refs[0]
    adds = refs[1:1 + n_add]
    g_ref, b_ref, o_ref, obf_ref = refs[1 + n_add:]
    h = alpha * x_ref[...]
    for a in adds:
        h = h + a[...]
    mu = jnp.mean(h, axis=-1, keepdims=True)
    c = h - mu
    var = jnp.mean(c * c, axis=-1, keepdims=True)
    y = c * lax.rsqrt(var + LN_EPS) * g_ref[...] + b_ref[...]
    o_ref[...] = y
    obf_ref[...] = y.astype(BF16)


def _ln_res(x, adds, g, b, alpha):
    n, d = x.shape
    tm = _tile(n, LN_TM, 16)
    row = pl.BlockSpec((tm, d), lambda i: (i, 0))
    vec = pl.BlockSpec((1, d), lambda i: (0, 0))
    return pl.pallas_call(
        functools.partial(_ln_body, n_add=len(adds), alpha=alpha),
        name="ln_res",
        grid=(n // tm,),
        in_specs=[row] * (1 + len(adds)) + [vec, vec],
        out_specs=[row, row],
        out_shape=[jax.ShapeDtypeStruct((n, d), F32), jax.ShapeDtypeStruct((n, d), BF16)],
        compiler_params=_cparams("parallel"),
    )(x, *adds, g.reshape(1, d), b.reshape(1, d))


def _gelu_tanh(y):
    return 0.5 * y * (1.0 + jnp.tanh(math.sqrt(2.0 / math.pi) * (y + 0.044715 * (y * y * y))))


def _s5_body(u_ref, bre_ref, bim_ref, cre_ref, cim_ref, lre_ref, lim_ref, d_ref, h0re_ref, h0im_ref,
             z_ref, zbf_ref, hre_ref, him_ref, car_re, car_im, *, t_rows, n_t):
    i = pl.program_id(2)

    @pl.when(i == 0)
    def _():
        car_re[...] = h0re_ref[0]
        car_im[...] = h0im_ref[0]

    u = u_ref[...]
    ub = u.astype(BF16)
    h_re = jnp.dot(ub, bre_ref[0], preferred_element_type=F32)
    h_im = jnp.dot(ub, bim_ref[0], preferred_element_type=F32)
    lr, li = lre_ref[0], lim_ref[0]
    cr, ci = car_re[...], car_im[...]
    row = lax.broadcasted_iota(jnp.int32, h_re.shape, 0)
    first = row == 0
    h_re = h_re + jnp.where(first, lr * cr - li * ci, 0.0)
    h_im = h_im + jnp.where(first, lr * ci + li * cr, 0.0)
    pr, pi = lr, li
    s = 1
    while s < t_rows:
        keep = row >= s
        sh_re = jnp.where(keep, pltpu.roll(h_re, s, 0), 0.0)
        sh_im = jnp.where(keep, pltpu.roll(h_im, s, 0), 0.0)
        h_re, h_im = h_re + pr * sh_re - pi * sh_im, h_im + pr * sh_im + pi * sh_re
        pr, pi = pr * pr - pi * pi, 2.0 * pr * pi
        s *= 2
    last_re = h_re[t_rows - 1:t_rows, :]
    last_im = h_im[t_rows - 1:t_rows, :]
    car_re[...] = last_re
    car_im[...] = last_im
    y = (jnp.dot(h_re.astype(BF16), cre_ref[0], preferred_element_type=F32)
         - jnp.dot(h_im.astype(BF16), cim_ref[0], preferred_element_type=F32))
    z = _gelu_tanh(y + d_ref[0] * u)
    z_ref[...] = z
    zbf_ref[...] = z.astype(BF16)

    @pl.when(i == n_t - 1)
    def _():
        hre_ref[0] = last_re
        him_ref[0] = last_im


def _s5_params(lam_re, lam_im, b_re, b_im, c_re, c_im, d, log_dt):
    g, p = lam_re.shape
    gs = b_re.shape[2]
    gpb = LANES // gs
    gb = g // gpb
    dt = jnp.exp(log_dt.astype(F32))[:, None]
    mag = jnp.exp(lam_re * dt)
    lb_re = mag * jnp.cos(lam_im * dt)
    lb_im = mag * jnp.sin(lam_im * dt)
    den = lam_re * lam_re + lam_im * lam_im
    q_re = ((lb_re - 1.0) * lam_re + lb_im * lam_im) / den
    q_im = (lb_im * lam_re - (lb_re - 1.0) * lam_im) / den
    bb_re = q_re[..., None] * b_re - q_im[..., None] * b_im
    bb_im = q_re[..., None] * b_im + q_im[..., None] * b_re
    eye = jnp.eye(gpb, dtype=F32)

    def pack_in(b):
        b = b.reshape(gb, gpb, p, gs).transpose(0, 1, 3, 2)
        return jnp.einsum('ab,xahp->xahbp', eye, b).reshape(gb, gpb * gs, gpb * p).astype(BF16)

    def pack_out(c):
        c = c.reshape(gb, gpb, gs, p).transpose(0, 1, 3, 2)
        return jnp.einsum('ab,xaph->xapbh', eye, c).reshape(gb, gpb * p, gpb * gs).astype(BF16)

    return dict(
        bre=pack_in(bb_re), bim=pack_in(bb_im), cre=pack_out(c_re.astype(F32)), cim=pack_out(c_im.astype(F32)),
        lre=lb_re.reshape(gb, 1, gpb * p), lim=lb_im.reshape(gb, 1, gpb * p),
        d=d.astype(F32).reshape(gb, 1, gpb * gs), gb=gb, s=gpb * p)


def _s5_scan(proj, row0, bsz, seq, prm, h0_re, h0_im):
    gb, s = prm['gb'], prm['s']
    t_rows = _tile(seq, S5_T, 8)
    n_t = seq // t_rows
    blk0 = row0 // t_rows
    assert row0 % t_rows == 0
    h0_re = h0_re.astype(F32).reshape(bsz * gb, 1, s)
    h0_im = h0_im.astype(F32).reshape(bsz * gb, 1, s)
    par = lambda shape: pl.BlockSpec((1,) + shape, lambda b, g, i: (g, 0, 0))
    st = pl.BlockSpec((1, 1, s), lambda b, g, i: (b * gb + g, 0, 0))
    zspec = pl.BlockSpec((t_rows, LANES), lambda b, g, i: (b * n_t + i, g))
    z, zbf, hre, him = pl.pallas_call(
        functools.partial(_s5_body, t_rows=t_rows, n_t=n_t),
        name="s5_scan",
        grid=(bsz, gb, n_t),
        in_specs=[pl.BlockSpec((t_rows, LANES), lambda b, g, i: (blk0 + b * n_t + i, g)),
                  par((LANES, s)), par((LANES, s)), par((s, LANES)), par((s, LANES)),
                  par((1, s)), par((1, s)), par((1, LANES)), st, st],
        out_specs=[zspec, zspec, st, st],
        out_shape=[jax.ShapeDtypeStruct((bsz * seq, gb * LANES), F32),
                   jax.ShapeDtypeStruct((bsz * seq, gb * LANES), BF16),
                   jax.ShapeDtypeStruct((bsz * gb, 1, s), F32),
                   jax.ShapeDtypeStruct((bsz * gb, 1, s), F32)],
        scratch_shapes=[pltpu.VMEM((1, s), F32), pltpu.VMEM((1, s), F32)],
        compiler_params=_cparams("parallel", "parallel", "arbitrary"),
    )(proj, prm['bre'], prm['bim'], prm['cre'], prm['cim'], prm['lre'], prm['lim'], prm['d'], h0_re, h0_im)
    return z, zbf, hre, him


def _softplus(x):
    return jnp.maximum(x, 0.0) + jnp.log1p(jnp.exp(-jnp.abs(x)))


def _rwkv_prep_body(pr_ref, pk_ref, pv_ref, pl_ref, sr_ref, sk_ref, sv_ref, sl_ref,
                    mr_ref, mk_ref, mv_ref, ml_ref, w0_ref, a0_ref, w2_ref, a2_ref, g2_ref,
                    r_ref, k_ref, v_ref, w_ref, a_ref, g_ref, cr, ck, cv, cl):
    i = pl.program_id(1)

    @pl.when(i == 0)
    def _():
        cr[...] = sr_ref[0]
        ck[...] = sk_ref[0]
        cv[...] = sv_ref[0]
        cl[...] = sl_ref[0]

    def mixed(p_ref, carry, mu_ref):
        x = p_ref[...]
        row = lax.broadcasted_iota(jnp.int32, x.shape, 0)
        prev = jnp.where(row == 0, carry[...], pltpu.roll(x, 1, 0))
        carry[...] = x[x.shape[0] - 1:, :]
        return x + (prev - x) * mu_ref[...]

    r_ref[...] = mixed(pr_ref, cr, mr_ref)
    k_ref[...] = mixed(pk_ref, ck, mk_ref)
    v_ref[...] = mixed(pv_ref, cv, mv_ref)
    lo = mixed(pl_ref, cl, ml_ref)
    tw = jnp.dot(jnp.tanh(lo).astype(BF16), w2_ref[...], preferred_element_type=F32)
    w_log = -_softplus(-(w0_ref[...] + tw)) - 0.5
    w_ref[...] = jnp.exp(-jnp.exp(w_log))
    a_ref[...] = jax.nn.sigmoid(a0_ref[...] + jnp.dot(lo.astype(BF16), a2_ref[...], preferred_element_type=F32))
    g_ref[...] = jnp.dot(jax.nn.sigmoid(lo).astype(BF16), g2_ref[...], preferred_element_type=F32)


def _rwkv_prep(proj, row0, bsz, seq, c, lp, col_r, shift, prm):
    t_rows = _tile(seq, RWKV_PREP_T, 8)
    n_t = seq // t_rows
    assert row0 % t_rows == 0 and col_r % c == 0 and (col_r + 3 * c) % lp == 0
    blk0 = row0 // t_rows
    cb = col_r // c
    lb = (col_r + 3 * c) // lp
    pin = lambda j: pl.BlockSpec((t_rows, c), lambda b, i: (blk0 + b * n_t + i, cb + j))
    sh = lambda w: pl.BlockSpec((1, 1, w), lambda b, i: (b, 0, 0))
    vec = lambda w: pl.BlockSpec((1, w), lambda b, i: (0, 0))
    mat = pl.BlockSpec((lp, c), lambda b, i: (0, 0))
    out = pl.BlockSpec((t_rows, c), lambda b, i: (b * n_t + i, 0))
    sds = jax.ShapeDtypeStruct((bsz * seq, c), F32)
    return pl.pallas_call(
        _rwkv_prep_body,
        name="rwkv_prep",
        grid=(bsz, n_t),
        in_specs=[pin(0), pin(1), pin(2),
                  pl.BlockSpec((t_rows, lp), lambda b, i: (blk0 + b * n_t + i, lb)),
                  sh(c), sh(c), sh(c), sh(lp), vec(c), vec(c), vec(c), vec(lp), vec(c), vec(c), mat, mat, mat],
        out_specs=[out] * 6,
        out_shape=[sds] * 6,
        scratch_shapes=[pltpu.VMEM((1, c), F32)] * 3 + [pltpu.VMEM((1, lp), F32)],
        compiler_params=_cparams("parallel", "arbitrary"),
    )(proj, proj, proj, proj, shift[0], shift[1], shift[2], shift[3],
      prm['mu_r'], prm['mu_k'], prm['mu_v'], prm['mu_l'], prm['w0'], prm['a0'], prm['w2'], prm['a2'], prm['g2'])


def _rwkv_scan_body(r_ref, k_ref, v_ref, w_ref, a_ref, kkp_ref, kap_ref, rkp_ref, lnw_ref, lnb_ref, s0_ref,
                    y_ref, sout_ref, st, kk_s, kka_s, km_s, wr_s, c1_s, c2_s, yr_s, *, t_rows, n_t, hd):
    i = pl.program_id(1)

    @pl.when(i == 0)
    def _():
        st[...] = s0_ref[...]

    r = r_ref[...]
    kraw = k_ref[...]
    a = a_ref[...]
    kk = kraw * kkp_ref[...][None]
    kk = kk * lax.rsqrt(jnp.maximum(jnp.sum(kk * kk, axis=1, keepdims=True), 1e-24))
    km = kraw * (1.0 + (a - 1.0) * kap_ref[...][None])
    kka = kk * a
    kk_s[...] = kk
    kka_s[...] = kka
    km_s[...] = km
    wr_s[...] = w_ref[...] * r
    c1_s[...] = jnp.sum(kka * r, axis=1, keepdims=True)
    c2_s[...] = jnp.sum(km * r, axis=1, keepdims=True)

    def step(t, carry):
        sk = jnp.zeros((hd, LANES), F32)
        yq = jnp.zeros((hd, LANES), F32)
        for j in range(hd):
            sj = st[j]
            sk = sk + sj * kk_s[t, pl.ds(j, 1), :]
            yq = yq + sj * wr_s[t, pl.ds(j, 1), :]
        v_t = v_ref[t]
        for j in range(hd):
            st[j] = (st[j] * w_ref[t, pl.ds(j, 1), :] - kka_s[t, pl.ds(j, 1), :] * sk
                     + km_s[t, pl.ds(j, 1), :] * v_t)
        yr_s[t] = yq - sk * c1_s[t] + v_t * c2_s[t]
        return carry

    lax.fori_loop(0, t_rows, step, 0)
    y = yr_s[...]
    ym = jnp.mean(y, axis=1, keepdims=True)
    yc = y - ym
    yv = jnp.mean(yc * yc, axis=1, keepdims=True)
    yn = yc * lax.rsqrt(yv + RWKV_GN_EPS) * lnw_ref[...][None] + lnb_ref[...][None]
    bonus = jnp.sum(r * km * rkp_ref[...][None], axis=1, keepdims=True) * v_ref[...]
    y_ref[...] = yn + bonus

    @pl.when(i == n_t - 1)
    def _():
        sout_ref[...] = st[...]


def _rwkv_scan(r, k, v, w, a, prm, s0):
    seq, hd, bh = r.shape
    t_rows = _tile(seq, RWKV_SCAN_T, 1)
    n_t = seq // t_rows
    n_lb = bh // LANES
    seq_spec = pl.BlockSpec((t_rows, hd, LANES), lambda l, i: (i, 0, l))
    par = pl.BlockSpec((hd, LANES), lambda l, i: (0, l))
    st_spec = pl.BlockSpec((hd, hd, LANES), lambda l, i: (0, 0, l))
    big = pltpu.VMEM((t_rows, hd, LANES), F32)
    small = pltpu.VMEM((t_rows, 1, LANES), F32)
    return pl.pallas_call(
        functools.partial(_rwkv_scan_body, t_rows=t_rows, n_t=n_t, hd=hd),
        name="rwkv_scan",
        grid=(n_lb, n_t),
        in_specs=[seq_spec] * 5 + [par] * 5 + [st_spec],
        out_specs=[seq_spec, st_spec],
        out_shape=[jax.ShapeDtypeStruct((seq, hd, bh), F32), jax.ShapeDtypeStruct((hd, hd, bh), F32)],
        scratch_shapes=[pltpu.VMEM((hd, hd, LANES), F32), big, big, big, big, small, small, big],
        compiler_params=_cparams("parallel", "arbitrary"),
    )(r, k, v, w, a, prm['k_k'], prm['k_a'], prm['r_k'], prm['ln_w'], prm['ln_b'], s0)


def _mul_cast_body(y_ref, g_ref, o_ref):
    o_ref[...] = (y_ref[...] * g_ref[...]).astype(o_ref.dtype)


def _mul_cast(y, g):
    n, d = y.shape
    tm = _tile(n, LN_TM * 2, 16)
    spec = pl.BlockSpec((tm, d), lambda i: (i, 0))
    return pl.pallas_call(
        _mul_cast_body, name="mul_cast", grid=(n // tm,), in_specs=[spec, spec], out_specs=spec,
        out_shape=jax.ShapeDtypeStruct((n, d), BF16), compiler_params=_cparams("parallel"),
    )(y, g)


def _rope_lanes(x, cos, sin, half):
    lane = lax.broadcasted_iota(jnp.int32, x.shape, 1)
    swapped = jnp.where(lane < half, pltpu.roll(x, LANES - half, 1), pltpu.roll(x, half, 1))
    return x * cos + swapped * sin


def _mla_prep_body(p_ref, qg_ref, kg_ref, cos_ref, sin_ref, cq_ref, cqt_ref, ckv_ref, ckr_ref, ckvt_ref, kr_ref,
                   *, ql, kl, half):
    x = p_ref[...]
    cq = x[:, :ql]
    cq = cq * lax.rsqrt(jnp.mean(cq * cq, axis=-1, keepdims=True) + RMS_EPS) * qg_ref[...]
    cq_ref[...] = cq.astype(BF16)
    cqt_ref[...] = cq.T.astype(BF16)
    ckv = x[:, ql:ql + kl]
    ckv = ckv * lax.rsqrt(jnp.mean(ckv * ckv, axis=-1, keepdims=True) + RMS_EPS) * kg_ref[...]
    ckv_ref[...] = ckv
    ckvt_ref[...] = ckv.T.astype(BF16)
    kr = _rope_lanes(x[:, ql + kl:], cos_ref[...], sin_ref[...], half)
    kr_ref[...] = kr
    ckr_ref[...] = jnp.concatenate([ckv, kr], axis=1).astype(BF16)


def _mla_prep(proj, q_norm, kv_norm, cos, sin, ql, kl, half):
    n, width = proj.shape
    tm = _tile(n, LN_TM, LANES)
    row = lambda w: pl.BlockSpec((tm, w), lambda i: (i, 0))
    col = lambda w: pl.BlockSpec((w, tm), lambda i: (0, i))
    vec = lambda w: pl.BlockSpec((1, w), lambda i: (0, 0))
    return pl.pallas_call(
        functools.partial(_mla_prep_body, ql=ql, kl=kl, half=half),
        name="mla_prep",
        grid=(n // tm,),
        in_specs=[row(width), vec(ql), vec(kl), row(LANES), row(LANES)],
        out_specs=[row(ql), col(ql), row(kl), row(kl + LANES), col(kl), row(LANES)],
        out_shape=[jax.ShapeDtypeStruct((n, ql), BF16), jax.ShapeDtypeStruct((ql, n), BF16),
                   jax.ShapeDtypeStruct((n, kl), F32), jax.ShapeDtypeStruct((n, kl + LANES), BF16),
                   jax.ShapeDtypeStruct((kl, n), BF16), jax.ShapeDtypeStruct((n, LANES), F32)],
        compiler_params=_cparams("parallel"),
    )(proj, q_norm.reshape(1, ql), kv_norm.reshape(1, kl), cos, sin)


def _online_softmax_update(s, v_blk, m_s, l_s, acc_s):
    m_prev = m_s[...]
    m_new = jnp.maximum(m_prev, jnp.max(s, axis=1, keepdims=True))
    alpha = jnp.exp(m_prev - m_new)
    p = jnp.exp(s - m_new)
    l_s[...] = alpha * l_s[...] + jnp.sum(p, axis=1, keepdims=True)
    acc_s[...] = alpha * acc_s[...] + jnp.dot(p.astype(BF16), v_blk, preferred_element_type=F32)
    m_s[...] = m_new


_NT = (((1,), (1,)), ((), ()))


def _attn_prompt_body(qt_ref, k_ref, vt_ref, o_ref, m_s, l_s, acc_s, *, tq, c):
    i = pl.program_id(2)
    m_s[...] = jnp.full(m_s.shape, MASK_VALUE, F32)
    l_s[...] = jnp.zeros(l_s.shape, F32)
    acc_s[...] = jnp.zeros(acc_s.shape, F32)
    qt = qt_ref[...]

    def update(j, masked):
        off = pl.multiple_of(j * tq, tq)
        s = jnp.dot(k_ref[pl.ds(off, tq), :], qt, preferred_element_type=F32)
        if masked:
            key = lax.broadcasted_iota(jnp.int32, s.shape, 0)
            qry = lax.broadcasted_iota(jnp.int32, s.shape, 1)
            s = jnp.where(key // CHUNK <= qry // CHUNK, s, MASK_VALUE)
        m_prev = m_s[...]
        m_new = jnp.maximum(m_prev, jnp.max(s, axis=0, keepdims=True))
        alpha = jnp.exp2((m_prev - m_new) * c)
        p = jnp.exp2((s - m_new) * c)
        l_s[...] = alpha * l_s[...] + jnp.sum(p, axis=0, keepdims=True)
        acc_s[...] = alpha * acc_s[...] + jnp.dot(vt_ref[:, pl.ds(off, tq)], p.astype(BF16),
                                                  preferred_element_type=F32)
        m_s[...] = m_new

    def full_tile(j, carry):
        update(j, False)
        return carry

    lax.fori_loop(0, i, full_tile, 0)
    update(i, True)
    o_ref[...] = (acc_s[...] / l_s[...]).T.astype(o_ref.dtype)


def _attn_prompt(qt, k, vt, bsz, seq, heads, scale):
    dk = qt.shape[0] // heads
    tq = _tile(seq, ATTN_TQ, LANES)
    assert tq % CHUNK == 0
    n_q = seq // tq
    return pl.pallas_call(
        functools.partial(_attn_prompt_body, tq=tq, c=scale * math.log2(math.e)),
        name="attn_prompt",
        grid=(bsz, heads, n_q),
        in_specs=[pl.BlockSpec((dk, tq), lambda b, h, i: (h, b * n_q + i)),
                  pl.BlockSpec((seq, dk), lambda b, h, i: (b, h)),
                  pl.BlockSpec((LANES, seq), lambda b, h, i: (h, b))],
        out_specs=pl.BlockSpec((tq, LANES), lambda b, h, i: (b * n_q + i, h)),
        out_shape=jax.ShapeDtypeStruct((bsz * seq, heads * LANES), BF16),
        scratch_shapes=[pltpu.VMEM((1, tq), F32), pltpu.VMEM((1, tq), F32), pltpu.VMEM((LANES, tq), F32)],
        compiler_params=_cparams("parallel", "parallel", "arbitrary"),
    )(qt, k, vt)


def _attn_latent_body(ql_ref, qr_ref, kv_ref, kr_ref, o_ref, m_s, l_s, acc_s,
                      *, tq, tk, n_k, heads, q_pos0, n_keys, scale):
    i = pl.program_id(1)
    m_s[...] = jnp.full(m_s.shape, MASK_VALUE, F32)
    l_s[...] = jnp.zeros(l_s.shape, F32)
    acc_s[...] = jnp.zeros(acc_s.shape, F32)
    ql = ql_ref[...]
    qr = qr_ref[...]

    def tile(j, carry):
        off = pl.multiple_of(j * tk, tk)
        kv = kv_ref[pl.ds(off, tk), :]
        s = lax.dot_general(ql, kv, _NT, preferred_element_type=F32)
        s = s + lax.dot_general(qr, kr_ref[pl.ds(off, tk), :], _NT, preferred_element_type=F32)
        row = lax.broadcasted_iota(jnp.int32, s.shape, 0)
        col = lax.broadcasted_iota(jnp.int32, s.shape, 1)
        q_pos = q_pos0 + (i * tq + row) // heads
        k_pos = off + col
        visible = (k_pos // CHUNK <= q_pos // CHUNK) & (k_pos < n_keys)
        s = jnp.where(visible, s * scale, MASK_VALUE)
        _online_softmax_update(s, kv, m_s, l_s, acc_s)
        return carry

    lax.fori_loop(0, n_k, tile, 0)
    o_ref[...] = (acc_s[...] / l_s[...]).astype(o_ref.dtype)


def _attn_latent(q_lat, q_rope, kv_all, kr_all, bsz, rows, heads, q_pos0, n_keys, scale):
    kl = q_lat.shape[1]
    nkp = kv_all.shape[0] // bsz
    tq = _tile(rows, LATENT_T, 16)
    tk = _tile(nkp, LATENT_T, 16)
    n_q = rows // tq
    return pl.pallas_call(
        functools.partial(_attn_latent_body, tq=tq, tk=tk, n_k=nkp // tk, heads=heads,
                          q_pos0=q_pos0, n_keys=n_keys, scale=scale),
        name="attn_latent",
        grid=(bsz, n_q),
        in_specs=[pl.BlockSpec((tq, kl), lambda b, i: (b * n_q + i, 0)),
                  pl.BlockSpec((tq, LANES), lambda b, i: (b * n_q + i, 0)),
                  pl.BlockSpec((nkp, kl), lambda b, i: (b, 0)),
                  pl.BlockSpec((nkp, LANES), lambda b, i: (b, 0))],
        out_specs=pl.BlockSpec((tq, kl), lambda b, i: (b * n_q + i, 0)),
        out_shape=jax.ShapeDtypeStruct((bsz * rows, kl), BF16),
        scratch_shapes=[pltpu.VMEM((tq, 1), F32), pltpu.VMEM((tq, 1), F32), pltpu.VMEM((tq, kl), F32)],
        compiler_params=_cparams("parallel", "arbitrary"),
    )(q_lat, q_rope, kv_all, kr_all)


def _router_body(x_ref, wh_ref, wl_ref, o_ref):
    x = x_ref[...]
    xh = x.astype(BF16)
    xl = (x - xh.astype(F32)).astype(BF16)
    wh = wh_ref[...]
    acc = jnp.dot(xh, wh, preferred_element_type=F32)
    acc = acc + jnp.dot(xl, wh, preferred_element_type=F32)
    acc = acc + jnp.dot(xh, wl_ref[...], preferred_element_type=F32)
    o_ref[...] = jax.nn.sigmoid(acc)


def _router(x, w):
    n, d = x.shape
    e = w.shape[1]
    tm = _tile(n, ROUTER_TM, 8)
    wh = w.astype(BF16)
    wl = (w - wh.astype(F32)).astype(BF16)
    wspec = pl.BlockSpec((d, e), lambda i: (0, 0))
    return pl.pallas_call(
        _router_body, name="moe_router", grid=(n // tm,),
        in_specs=[pl.BlockSpec((tm, d), lambda i: (i, 0)), wspec, wspec],
        out_specs=pl.BlockSpec((tm, e), lambda i: (i, 0)),
        out_shape=jax.ShapeDtypeStruct((n, e), F32),
        compiler_params=_cparams("parallel"),
    )(x, wh, wl)


def _route_body(s_ref, b_ref, idx_ref, w_ref, sel_ref):
    s = s_ref[...]
    t, e = s.shape
    gsz = e // N_GROUP
    biased = s + b_ref[...]
    lane = lax.broadcasted_iota(jnp.int32, (t, e), 1)
    grp = lane // gsz
    neg = -jnp.inf
    gscore = []
    for g in range(N_GROUP):
        v = jnp.where(grp == g, biased, neg)
        m1 = jnp.max(v, axis=1, keepdims=True)
        first = jnp.min(jnp.where(v == m1, lane, e), axis=1, keepdims=True)
        m2 = jnp.max(jnp.where(lane == first, neg, v), axis=1, keepdims=True)
        gscore.append(m1 + m2)
    taken = [jnp.zeros((t, 1), jnp.bool_) for _ in range(N_GROUP)]
    for _ in range(TOPK_GROUP):
        avail = [jnp.where(taken[g], neg, gscore[g]) for g in range(N_GROUP)]
        best = avail[0]
        for g in range(1, N_GROUP):
            best = jnp.maximum(best, avail[g])
        found = jnp.zeros((t, 1), jnp.bool_)
        for g in range(N_GROUP):
            pick = jnp.logical_and(jnp.logical_not(found), avail[g] == best)
            pick = jnp.logical_and(pick, jnp.logical_not(taken[g]))
            taken[g] = jnp.logical_or(taken[g], pick)
            found = jnp.logical_or(found, pick)
    keep = jnp.zeros((t, e), jnp.bool_)
    for g in range(N_GROUP):
        keep = jnp.logical_or(keep, jnp.logical_and(grp == g, taken[g]))
    cand = jnp.where(keep, biased, neg)
    idx_out = jnp.zeros((t, e), jnp.int32)
    w_out = jnp.zeros((t, e), F32)
    sel = jnp.zeros((t, e), jnp.bool_)
    w_sum = jnp.zeros((t, 1), F32)
    for k in range(TOP_K):
        m = jnp.max(cand, axis=1, keepdims=True)
        pos = jnp.min(jnp.where(cand == m, lane, e), axis=1, keepdims=True)
        hit = lane == pos
        w_k = jnp.sum(jnp.where(hit, s, 0.0), axis=1, keepdims=True)
        cand = jnp.where(hit, neg, cand)
        sel = jnp.logical_or(sel, hit)
        idx_out = jnp.where(lane == k, pos, idx_out)
        w_out = jnp.where(lane == k, w_k, w_out)
        w_sum = w_sum + w_k
    idx_ref[...] = idx_out
    w_ref[...] = w_out / (w_sum + 1e-20) * ROUTED_SCALE
    sel_ref[...] = jnp.where(sel, 1.0, 0.0).astype(BF16)


def _route(scores, bias):
    n, e = scores.shape
    t = _tile(n, ROUTER_TM, 16)
    spec = pl.BlockSpec((t, e), lambda i: (i, 0))
    return pl.pallas_call(
        _route_body, name="moe_route", grid=(n // t,),
        in_specs=[spec, pl.BlockSpec((1, e), lambda i: (0, 0))],
        out_specs=[spec, spec, spec],
        out_shape=[jax.ShapeDtypeStruct((n, e), jnp.int32), jax.ShapeDtypeStruct((n, e), F32),
                   jax.ShapeDtypeStruct((n, e), BF16)],
        compiler_params=_cparams("parallel"),
    )(scores, bias.astype(F32).reshape(1, e))


def _rank_body(sel_ref, cum_ref, cnt_ref, carry):
    @pl.when(pl.program_id(0) == 0)
    def _():
        carry[...] = jnp.zeros(carry.shape, F32)

    sel = sel_ref[...]
    t = sel.shape[0]
    earlier = (lax.broadcasted_iota(jnp.int32, (t, t), 1) < lax.broadcasted_iota(jnp.int32, (t, t), 0))
    cum = jnp.dot(earlier.astype(BF16), sel, preferred_element_type=F32) + carry[...]
    cum_ref[...] = cum.astype(jnp.int32)
    carry[...] = carry[...] + jnp.sum(sel.astype(F32), axis=0, keepdims=True)
    cnt_ref[...] = carry[...].astype(jnp.int32)


def _rank(sel):
    n, e = sel.shape
    t = _tile(n, ROUTER_TM, 16)
    return pl.pallas_call(
        _rank_body, name="moe_rank", grid=(n // t,),
        in_specs=[pl.BlockSpec((t, e), lambda i: (i, 0))],
        out_specs=[pl.BlockSpec((t, e), lambda i: (i, 0)), pl.BlockSpec((1, e), lambda i: (0, 0))],
        out_shape=[jax.ShapeDtypeStruct((n, e), jnp.int32), jax.ShapeDtypeStruct((1, e), jnp.int32)],
        scratch_shapes=[pltpu.VMEM((1, e), F32)],
        compiler_params=_cparams("arbitrary"),
    )(sel)


def _moe_block(b, be_ref, nu_ref):
    bb = jnp.minimum(b, nu_ref[0] - 1)
    changed = (b == 0) | (be_ref[bb] != be_ref[jnp.maximum(bb - 1, 0)])
    return b < nu_ref[0], changed


def _moe_up_body(be_ref, nu_ref, x_ref, wg_ref, wu_ref, o_ref, wgb, wub):
    active, changed = _moe_block(pl.program_id(1), be_ref, nu_ref)

    @pl.when(active & changed)
    def _():
        wgb[...] = wg_ref[0, 0].astype(BF16)
        wub[...] = wu_ref[0, 0].astype(BF16)

    @pl.when(active)
    def _():
        x = x_ref[...]
        h1 = jnp.dot(x, wgb[...], preferred_element_type=F32)
        h2 = jnp.dot(x, wub[...], preferred_element_type=F32)
        o_ref[...] = (h1 * jax.nn.sigmoid(h1) * h2).astype(o_ref.dtype)


def _moe_down_body(be_ref, nu_ref, h_ref, wd_ref, o_ref, wdb):
    active, changed = _moe_block(pl.program_id(1), be_ref, nu_ref)

    @pl.when(active & changed)
    def _():
        wdb[...] = wd_ref[0, 0].astype(BF16)

    @pl.when(active)
    def _():
        o_ref[...] = jnp.dot(h_ref[...], wdb[...], preferred_element_type=F32)


def _expert_ffn(xs, blk_expert, n_used, w_gate, w_up, w_down, layer, tm):
    p, d = xs.shape
    ff = w_gate.shape[3]
    n_blk = p // tm
    tf = _tile(ff, MOE_TF, LANES)
    tn = _tile(d, MOE_TN, LANES)
    blk = lambda b, nu: jnp.minimum(b, nu[0] - 1)
    hid = pl.pallas_call(
        _moe_up_body, name="moe_up",
        grid_spec=pltpu.PrefetchScalarGridSpec(
            num_scalar_prefetch=2, grid=(ff // tf, n_blk),
            in_specs=[pl.BlockSpec((tm, d), lambda f, b, be, nu: (blk(b, nu), 0)),
                      pl.BlockSpec((1, 1, d, tf), lambda f, b, be, nu: (layer, be[blk(b, nu)], 0, f)),
                      pl.BlockSpec((1, 1, d, tf), lambda f, b, be, nu: (layer, be[blk(b, nu)], 0, f))],
            out_specs=pl.BlockSpec((tm, tf), lambda f, b, be, nu: (blk(b, nu), f)),
            scratch_shapes=[pltpu.VMEM((d, tf), BF16), pltpu.VMEM((d, tf), BF16)]),
        out_shape=jax.ShapeDtypeStruct((p, ff), BF16),
        compiler_params=_cparams("arbitrary", "arbitrary"),
    )(blk_expert, n_used, xs, w_gate, w_up)
    return pl.pallas_call(
        _moe_down_body, name="moe_down",
        grid_spec=pltpu.PrefetchScalarGridSpec(
            num_scalar_prefetch=2, grid=(d // tn, n_blk),
            in_specs=[pl.BlockSpec((tm, ff), lambda n, b, be, nu: (blk(b, nu), 0)),
                      pl.BlockSpec((1, 1, ff, tn), lambda n, b, be, nu: (layer, be[blk(b, nu)], 0, n))],
            out_specs=pl.BlockSpec((tm, tn), lambda n, b, be, nu: (blk(b, nu), n)),
            scratch_shapes=[pltpu.VMEM((ff, tn), BF16)]),
        out_shape=jax.ShapeDtypeStruct((p, d), F32),
        compiler_params=_cparams("arbitrary", "arbitrary"),
    )(blk_expert, n_used, hid, w_down)


def _moe_ffn(x, x_bf, layer, w_router, b_router, w_gate, w_up, w_down, ws_gate, ws_up, ws_down):
    n_tok, d = x.shape
    n_exp = w_router.shape[2]
    scores = _router(x, w_router[layer].astype(F32))
    top_idx, top_w, sel = _route(scores, b_router[layer])
    top_idx, top_w = top_idx[:, :TOP_K], top_w[:, :TOP_K]
    n_asg = n_tok * TOP_K
    tm = _tile(n_tok, MOE_TM, 16)
    n_blk = -(-n_asg // tm) + n_exp
    before, counts = _rank(sel)
    counts = counts[0]
    padded = (counts + tm - 1) // tm * tm
    pad_end = jnp.cumsum(padded)
    pad_start = pad_end - padded
    dest = (pad_start[top_idx] + jnp.take_along_axis(before, top_idx, axis=-1)).astype(jnp.int32)
    tok = jnp.broadcast_to(jnp.arange(n_tok, dtype=jnp.int32)[:, None], dest.shape)
    slot_tok = jnp.full((n_blk * tm,), n_tok, jnp.int32).at[dest.reshape(-1)].set(tok.reshape(-1))
    blk_expert = jnp.minimum(jnp.searchsorted(pad_end, jnp.arange(n_blk) * tm, side='right'),
                             n_exp - 1).astype(jnp.int32)
    n_used = (pad_end[-1] // tm).astype(jnp.int32).reshape(1)
    filled = (slot_tok < n_tok)[:, None]
    xs = jnp.where(filled, x_bf[jnp.minimum(slot_tok, n_tok - 1)], jnp.zeros((), BF16))
    y_slots = _expert_ffn(xs, blk_expert, n_used, w_gate, w_up, w_down, layer, tm)
    routed = (y_slots[dest] * top_w[:, :, None]).sum(axis=1)
    n_sblk = n_tok // tm
    shared = _expert_ffn(x_bf, jnp.zeros((n_sblk,), jnp.int32), jnp.full((1,), n_sblk, jnp.int32),
                         ws_gate[:, None], ws_up[:, None], ws_down[:, None], layer, tm)
    return routed, shared


def _to_scan_layout(t, bsz, seq, heads, hd, bh_pad):
    t = t.reshape(bsz, seq, heads, hd).transpose(1, 3, 0, 2).reshape(seq, hd, bsz * heads)
    return jnp.pad(t, ((0, 0), (0, 0), (0, bh_pad - bsz * heads)))


def _rope_tables(pos, half):
    inv = jnp.exp(-math.log(ROPE_THETA) * jnp.arange(half, dtype=F32) / half)
    ang = pos.astype(F32)[:, None] * inv[None, :]
    cos, sin = jnp.cos(ang), jnp.sin(ang)
    pad = LANES - 2 * half
    ones = jnp.ones((pos.shape[0], pad), F32)
    zeros = jnp.zeros((pos.shape[0], pad), F32)
    return jnp.concatenate([cos, cos, ones], 1), jnp.concatenate([-sin, sin, zeros], 1)


def kernel(x_prompt, x_sample, state_s5_re, state_s5_im, state_rwkv, state_rwkv_shift, cache_c_kv, cache_k_rope, l0_w_in, s5_lambda_re, s5_lambda_im, s5_b_re, s5_b_im, s5_c_re, s5_c_im, s5_d, s5_log_dt, s5_w_glu, s5_b_glu, rwkv_mu, rwkv_w0, rwkv_w2, rwkv_a0, rwkv_a2, rwkv_g2, rwkv_k_k, rwkv_k_a, rwkv_r_k, rwkv_ln_w, rwkv_ln_b, l0_w_out, l1_w_in, mla_q_norm, mla_w_uq, mla_kv_norm, mla_w_uk, mla_w_uv, l1_w_out, ln_mix_g, ln_mix_b, ln_ffn_g, ln_ffn_b, moe_w_router, moe_b_router, moe_w_gate, moe_w_up, moe_w_down, moe_ws_gate, moe_ws_up, moe_ws_down):
    bp, lp_, d = x_prompt.shape
    bs, ls, _ = x_sample.shape
    n_p, n_s = bp * lp_, bs * ls
    depth = ln_mix_g.shape[0]
    alpha = (2 * depth) ** 0.25
    groups = (("p", bp, lp_, 0), ("s", bs, ls, n_p))

    x = jnp.concatenate([x_prompt.reshape(n_p, d), x_sample.reshape(n_s, d)], axis=0).astype(F32)
    x_bf = x.astype(BF16)
    moe_p = (moe_w_router, moe_b_router, moe_w_gate, moe_w_up, moe_w_down, moe_ws_gate, moe_ws_up, moe_ws_down)

    s5_g, s5_p = s5_lambda_re.shape
    ws = s5_g * s5_b_re.shape[2]
    heads_r, hd = rwkv_r_k.shape
    c = heads_r * hd
    lora_w, lora_a, lora_g = rwkv_w2.shape[0], rwkv_a2.shape[0], rwkv_g2.shape[0]
    lora = lora_w + lora_a + lora_g
    lp = -(-lora // LANES) * LANES
    proj_w = ws + 3 * c + lora
    w_in = jnp.pad(l0_w_in, ((0, 0), (0, lp - lora))).astype(BF16)
    proj = _mm(x_bf, w_in, tn=512, name="mm_l0_in")
    s5p = _s5_params(s5_lambda_re.astype(F32), s5_lambda_im.astype(F32), s5_b_re.astype(F32),
                     s5_b_im.astype(F32), s5_c_re, s5_c_im, s5_d, s5_log_dt)
    mu = jnp.pad(rwkv_mu.astype(F32), (0, lp - lora)).reshape(1, -1)
    row_pad = lambda m, at: jnp.pad(m.astype(F32), ((at, lp - at - m.shape[0]), (0, 0))).astype(BF16)
    prep_p = dict(mu_r=mu[:, :c], mu_k=mu[:, c:2 * c], mu_v=mu[:, 2 * c:3 * c], mu_l=mu[:, 3 * c:],
                  w0=rwkv_w0.astype(F32).reshape(1, c), a0=rwkv_a0.astype(F32).reshape(1, c),
                  w2=row_pad(rwkv_w2, 0), a2=row_pad(rwkv_a2, lora_w), g2=row_pad(rwkv_g2, lora_w + lora_a))

    zs, zbfs, yrs, outs0 = [], [], [], {}
    for name, bsz, seq, row0 in groups:
        if name == "p":
            h0re = jnp.zeros((bsz, s5_g, s5_p), F32)
            h0im = h0re
            s0 = jnp.zeros((bsz, heads_r, hd, hd), F32)
            shift0 = jnp.zeros((bsz, proj_w - ws), F32)
        else:
            h0re, h0im, s0, shift0 = state_s5_re, state_s5_im, state_rwkv.astype(F32), state_rwkv_shift.astype(F32)
        z, zbf, hre, him = _s5_scan(proj, row0, bsz, seq, s5p, h0re, h0im)
        zs.append(z)
        zbfs.append(zbf)
        shift0 = jnp.pad(shift0, ((0, 0), (0, lp - lora)))[:, None, :]
        shifts = (shift0[..., :c], shift0[..., c:2 * c], shift0[..., 2 * c:3 * c], shift0[..., 3 * c:])
        r, k, v, w, a, g = _rwkv_prep(proj, row0, bsz, seq, c, lp, ws, shifts, prep_p)
        bh = bsz * heads_r
        bh_pad = -(-bh // LANES) * LANES
        lay = lambda t: _to_scan_layout(t, bsz, seq, heads_r, hd, bh_pad)
        par = lambda t: jnp.pad(jnp.tile(t.astype(F32).reshape(heads_r, hd).T, (1, bsz)), ((0, 0), (0, bh_pad - bh)))
        scan_p = dict(k_k=par(rwkv_k_k), k_a=par(rwkv_k_a), r_k=par(rwkv_r_k), ln_w=par(rwkv_ln_w), ln_b=par(rwkv_ln_b))
        s0l = jnp.pad(s0.transpose(3, 2, 0, 1).reshape(hd, hd, bh), ((0, 0), (0, 0), (0, bh_pad - bh)))
        y_scan, s_last = _rwkv_scan(lay(r), lay(k), lay(v), lay(w), lay(a), scan_p, s0l)
        y_nat = y_scan[:, :, :bh].reshape(seq, hd, bsz, heads_r).transpose(2, 0, 3, 1).reshape(bsz * seq, c)
        yrs.append(_mul_cast(y_nat, g))
        last = jnp.stack([lax.dynamic_slice(proj, (row0 + (b + 1) * seq - 1, ws), (1, proj_w - ws))[0]
                          for b in range(bsz)])
        outs0[name] = (hre.reshape(bsz, s5_g, s5_p).astype(state_s5_re.dtype),
                       him.reshape(bsz, s5_g, s5_p).astype(state_s5_im.dtype),
                       s_last[:, :, :bh].reshape(hd, hd, bsz, heads_r).transpose(2, 3, 1, 0).astype(state_rwkv.dtype),
                       last.astype(state_rwkv_shift.dtype))
    z = jnp.concatenate(zs, axis=0)
    zbf = jnp.concatenate(zbfs, axis=0)
    y_a = _mm(zbf, s5_w_glu.astype(BF16), out_dtype=BF16, name="mm_glu",
              epilogue=lambda acc, zt, b: zt * jax.nn.sigmoid(acc + b),
              extras=((z, 'tile'), (s5_b_glu.astype(F32).reshape(1, ws), 'row')))
    mixed = jnp.concatenate([y_a, jnp.concatenate(yrs, axis=0)], axis=1)
    mix = _mm(mixed, l0_w_out.astype(BF16), name="mm_l0_out")
    x, x_bf = _ln_res(x, [mix], ln_mix_g[0].astype(F32), ln_mix_b[0].astype(F32), alpha)
    routed, shared = _moe_ffn(x, x_bf, 0, *moe_p)
    x, x_bf = _ln_res(x, [routed, shared], ln_ffn_g[0].astype(F32), ln_ffn_b[0].astype(F32), alpha)

    ql, heads_a, qd = mla_w_uq.shape
    kl, _, nope = mla_w_uk.shape
    vd = mla_w_uv.shape[2]
    rope = qd - nope
    half = rope // 2
    past = cache_c_kv.shape[1]
    assert nope == LANES and vd == LANES and rope <= LANES
    scale = float(qd) ** -0.5
    w1 = jnp.pad(l1_w_in, ((0, 0), (0, LANES - rope))).astype(BF16)
    proj1 = _mm(x_bf, w1, tm=512, tn=ql + kl + LANES, name="mm_l1_in")
    pos = jnp.concatenate([jnp.tile(jnp.arange(lp_, dtype=jnp.int32), bp),
                           jnp.tile(past + jnp.arange(ls, dtype=jnp.int32), bs)])
    cos, sin = _rope_tables(pos, half)
    cqn, cqn_t, ckv, ckr, ckv_t, kr = _mla_prep(proj1, mla_q_norm.astype(F32), mla_kv_norm.astype(F32),
                                                cos, sin, ql, kl, half)
    w_qn3 = mla_w_uq[:, :, :nope]
    w_qr3 = jnp.pad(mla_w_uq[:, :, nope:], ((0, 0), (0, 0), (0, LANES - rope)))
    w_qt = jnp.stack([w_qn3, w_qr3], axis=2).reshape(ql, heads_a * 2 * LANES).T.astype(BF16)

    def rope_rows_epilogue(acc, cos_t, sin_t):
        out = []
        for j in range(acc.shape[0] // LANES):
            blk = acc[j * LANES:(j + 1) * LANES, :]
            if j % 2 == 1:
                row = lax.broadcasted_iota(jnp.int32, blk.shape, 0)
                swapped = jnp.where(row < half, pltpu.roll(blk, LANES - half, 0), pltpu.roll(blk, half, 0))
                blk = blk * cos_t + swapped * sin_t
            out.append(blk)
        return jnp.concatenate(out, axis=0)

    q_t = _mm(w_qt, cqn_t, n=n_p, out_dtype=BF16, epilogue=rope_rows_epilogue, name="mm_qt",
              extras=((cos[:n_p].T, 'coltile'), (sin[:n_p].T, 'coltile')))
    eye_rows = jnp.broadcast_to(jnp.eye(LANES, dtype=F32)[:, None, :], (LANES, heads_a, LANES))
    w_k = jnp.concatenate([jnp.stack([mla_w_uk.astype(F32), jnp.zeros((kl, heads_a, LANES), F32)], axis=2),
                           jnp.stack([jnp.zeros((LANES, heads_a, nope), F32), eye_rows], axis=2)], axis=0)
    k_cat = _mm(ckr, w_k.reshape(kl + LANES, heads_a * 2 * LANES).astype(BF16), m=n_p, out_dtype=BF16, name="mm_kcat")
    v_t = _mm(mla_w_uv.reshape(kl, heads_a * vd).T.astype(BF16), ckv_t, n=n_p, out_dtype=BF16, name="mm_vt")
    o_p = _attn_prompt(q_t, k_cat, v_t, bp, lp_, heads_a, scale)
    cqn_s = cqn[n_p:]

    def rope_epilogue(acc, cos_t, sin_t):
        return jnp.concatenate([_rope_lanes(acc[:, j * LANES:(j + 1) * LANES], cos_t, sin_t, half)
                                for j in range(acc.shape[1] // LANES)], axis=1)

    q_nope_s = _mm(cqn_s, w_qn3.reshape(ql, heads_a * nope).astype(BF16), out_dtype=BF16)
    q_rope_s = _mm(cqn_s, w_qr3.reshape(ql, heads_a * LANES).astype(BF16), out_dtype=BF16, epilogue=rope_epilogue,
                   extras=((cos[n_p:], 'rowtile'), (sin[n_p:], 'rowtile')))
    q_lat = _bmm_heads(q_nope_s, mla_w_uk.transpose(1, 2, 0).astype(BF16))
    n_keys = past + ls
    nkp = -(-n_keys // LATENT_T) * LATENT_T
    kv_all = jnp.concatenate([cache_c_kv.astype(F32), ckv[n_p:].reshape(bs, ls, kl)], axis=1)
    kr_all = jnp.concatenate([jnp.pad(cache_k_rope.astype(F32), ((0, 0), (0, 0), (0, LANES - rope))),
                              kr[n_p:].reshape(bs, ls, LANES)], axis=1)
    padk = lambda t: jnp.pad(t, ((0, 0), (0, nkp - n_keys), (0, 0))).astype(BF16).reshape(bs * nkp, -1)
    o_lat = _attn_latent(q_lat.reshape(n_s * heads_a, kl), q_rope_s.reshape(n_s * heads_a, LANES),
                         padk(kv_all), padk(kr_all), bs, ls * heads_a, heads_a, past, n_keys, scale)
    o_s = _bmm_heads(o_lat.reshape(n_s, heads_a * kl), mla_w_uv.transpose(1, 0, 2).astype(BF16))
    mix = _mm(jnp.concatenate([o_p, o_s], axis=0), l1_w_out.astype(BF16), name="mm_l1_out")
    x, x_bf = _ln_res(x, [mix], ln_mix_g[1].astype(F32), ln_mix_b[1].astype(F32), alpha)
    routed, shared = _moe_ffn(x, x_bf, 1, *moe_p)
    x, _ = _ln_res(x, [routed, shared], ln_ffn_g[1].astype(F32), ln_ffn_b[1].astype(F32), alpha)

    dt = x_prompt.dtype
    y_prompt = x[:n_p].reshape(bp, lp_, d).astype(dt)
    y_sample = x[n_p:].reshape(bs, ls, d).astype(dt)
    ckv_p = ckv[:n_p].reshape(bp, lp_, kl).astype(dt)
    ckv_s = ckv[n_p:].reshape(bs, ls, kl).astype(dt)
    kr_p = kr[:n_p, :rope].reshape(bp, lp_, rope).astype(dt)
    kr_s = kr[n_p:, :rope].reshape(bs, ls, rope).astype(dt)
    return (y_prompt, y_sample, *outs0["p"], ckv_p, kr_p, *outs0["s"], ckv_s, kr_s)
```

```python
import functools
import math

import jax
import jax.numpy as jnp
from jax import lax
from jax.experimental import pallas as pl
from jax.experimental.pallas import tpu as pltpu

F32 = jnp.float32
BF16 = jnp.bfloat16

CHUNK = 64
LN_EPS = 1e-5
RMS_EPS = 1e-6
RWKV_GN_EPS = 64e-5
ROPE_THETA = 10000.0
TOP_K = 8
N_GROUP = 8
TOPK_GROUP = 4
ROUTED_SCALE = 2.5
MASK_VALUE = -1e30

LANES = 128
V7X_VMEM_BYTES = 64 * 2**20
VMEM_LIMIT = V7X_VMEM_BYTES - 8 * 2**20

MM_TM = 1024
MM_TN = 1024
MM_TK = 4096
LN_TM = 256
S5_T = 512
RWKV_PREP_T = 256
RWKV_SCAN_T = 64
ATTN_TQ = 1024
LATENT_T = 512
MOE_TM = 256
MOE_TF = 512
MOE_TN = 2048
ROUTER_TM = 512


def _tile(n, target, align):
    if n <= target:
        return n
    t = (target // align) * align
    while t >= align:
        if n % t == 0:
            return t
        t -= align
    return n


def _cparams(*sems):
    return pltpu.CompilerParams(dimension_semantics=sems, vmem_limit_bytes=VMEM_LIMIT)


def _mm_body(*refs, nk, n_extra, epilogue):
    x_ref, w_ref = refs[0], refs[1]
    extras = refs[2:2 + n_extra]
    o_ref = refs[2 + n_extra]
    part = jnp.dot(x_ref[...], w_ref[...], preferred_element_type=F32)

    def finish(acc):
        if epilogue is not None:
            acc = epilogue(acc, *[e[...] for e in extras])
        o_ref[...] = acc.astype(o_ref.dtype)

    if nk == 1:
        finish(part)
        return
    acc_ref = refs[3 + n_extra]
    k = pl.program_id(2)

    @pl.when(k == 0)
    def _():
        acc_ref[...] = part

    @pl.when(k > 0)
    def _():
        acc_ref[...] += part

    @pl.when(k == nk - 1)
    def _():
        finish(acc_ref[...])


def _mm(x, w, *, out_dtype=F32, tm=None, tn=None, tk=None, epilogue=None, extras=(), m=None, n=None, name="mm"):
    k = x.shape[1]
    m = m or x.shape[0]
    n = n or w.shape[1]
    tm = _tile(m, tm or MM_TM, 16)
    tn = _tile(n, tn or MM_TN, LANES)
    tk = _tile(k, tk or MM_TK, LANES)
    nk = k // tk
    in_specs = [pl.BlockSpec((tm, tk), lambda i, j, kk: (i, kk)),
                pl.BlockSpec((tk, tn), lambda i, j, kk: (kk, j))]
    args = [x, w]
    for arr, kind in extras:
        if kind == 'tile':
            in_specs.append(pl.BlockSpec((tm, tn), lambda i, j, kk: (i, j)))
        elif kind == 'row':
            in_specs.append(pl.BlockSpec((1, tn), lambda i, j, kk: (0, j)))
        elif kind == 'coltile':
            in_specs.append(pl.BlockSpec((LANES, tn), lambda i, j, kk: (0, j)))
        else:
            in_specs.append(pl.BlockSpec((tm, LANES), lambda i, j, kk: (i, 0)))
        args.append(arr)
    scratch = [pltpu.VMEM((tm, tn), F32)] if nk > 1 else []
    return pl.pallas_call(
        functools.partial(_mm_body, nk=nk, n_extra=len(extras), epilogue=epilogue),
        name=name,
        grid=(m // tm, n // tn, nk),
        in_specs=in_specs,
        out_specs=pl.BlockSpec((tm, tn), lambda i, j, kk: (i, j)),
        out_shape=jax.ShapeDtypeStruct((m, n), out_dtype),
        scratch_shapes=scratch,
        compiler_params=_cparams("parallel", "parallel", "arbitrary"),
    )(*args)


def _bmm_heads_body(x_ref, w_ref, o_ref):
    o_ref[...] = jnp.dot(x_ref[...], w_ref[0], preferred_element_type=F32).astype(o_ref.dtype)


def _bmm_heads(x, w, out_dtype=BF16):
    m = x.shape[0]
    h, din, dout = w.shape
    return pl.pallas_call(
        _bmm_heads_body,
        name="bmm_heads",
        grid=(h,),
        in_specs=[pl.BlockSpec((m, din), lambda i: (0, i)),
                  pl.BlockSpec((1, din, dout), lambda i: (i, 0, 0))],
        out_specs=pl.BlockSpec((m, dout), lambda i: (0, i)),
        out_shape=jax.ShapeDtypeStruct((m, h * dout), out_dtype),
        compiler_params=_cparams("parallel"),
    )(x, w)


def _ln_body(*refs, n_add, alpha):
    x_ref = refs[0]
    adds = refs[1:1 + n_add]
    g_ref, b_ref, o_ref, obf_ref = refs[1 + n_add:]
    h = alpha * x_ref[...]
    for a in adds:
        h = h + a[...]
    mu = jnp.mean(h, axis=-1, keepdims=True)
    c = h - mu
    var = jnp.mean(c * c, axis=-1, keepdims=True)
    y = c * lax.rsqrt(var + LN_EPS) * g_ref[...] + b_ref[...]
    o_ref[...] = y
    obf_ref[...] = y.astype(BF16)


def _ln_res(x, adds, g, b, alpha):
    n, d = x.shape
    tm = _tile(n, LN_TM, 16)
    row = pl.BlockSpec((tm, d), lambda i: (i, 0))
    vec = pl.BlockSpec((1, d), lambda i: (0, 0))
    return pl.pallas_call(
        functools.partial(_ln_body, n_add=len(adds), alpha=alpha),
        name="ln_res",
        grid=(n // tm,),
        in_specs=[row] * (1 + len(adds)) + [vec, vec],
        out_specs=[row, row],
        out_shape=[jax.ShapeDtypeStruct((n, d), F32), jax.ShapeDtypeStruct((n, d), BF16)],
        compiler_params=_cparams("parallel"),
    )(x, *adds, g.reshape(1, d), b.reshape(1, d))


def _gelu_tanh(y):
    return 0.5 * y * (1.0 + jnp.tanh(math.sqrt(2.0 / math.pi) * (y + 0.044715 * (y * y * y))))


def _s5_body(u_ref, bre_ref, bim_ref, cre_ref, cim_ref, lre_ref, lim_ref, d_ref, h0re_ref, h0im_ref,
             z_ref, zbf_ref, hre_ref, him_ref, car_re, car_im, *, t_rows, n_t):
    i = pl.program_id(2)

    @pl.when(i == 0)
    def _():
        car_re[...] = h0re_ref[0]
        car_im[...] = h0im_ref[0]

    u = u_ref[...]
    ub = u.astype(BF16)
    h_re = jnp.dot(ub, bre_ref[0], preferred_element_type=F32)
    h_im = jnp.dot(ub, bim_ref[0], preferred_element_type=F32)
    lr, li = lre_ref[0], lim_ref[0]
    cr, ci = car_re[...], car_im[...]
    row = lax.broadcasted_iota(jnp.int32, h_re.shape, 0)
    first = row == 0
    h_re = h_re + jnp.where(first, lr * cr - li * ci, 0.0)
    h_im = h_im + jnp.where(first, lr * ci + li * cr, 0.0)
    pr, pi = lr, li
    s = 1
    while s < t_rows:
        keep = row >= s
        sh_re = jnp.where(keep, pltpu.roll(h_re, s, 0), 0.0)
        sh_im = jnp.where(keep, pltpu.roll(h_im, s, 0), 0.0)
        h_re, h_im = h_re + pr * sh_re - pi * sh_im, h_im + pr * sh_im + pi * sh_re
        pr, pi = pr * pr - pi * pi, 2.0 * pr * pi
        s *= 2
    last_re = h_re[t_rows - 1:t_rows, :]
    last_im = h_im[t_rows - 1:t_rows, :]
    car_re[...] = last_re
    car_im[...] = last_im
    y = (jnp.dot(h_re.astype(BF16), cre_ref[0], preferred_element_type=F32)
         - jnp.dot(h_im.astype(BF16), cim_ref[0], preferred_element_type=F32))
    z = _gelu_tanh(y + d_ref[0] * u)
    z_ref[...] = z
    zbf_ref[...] = z.astype(BF16)

    @pl.when(i == n_t - 1)
    def _():
        hre_ref[0] = last_re
        him_ref[0] = last_im


def _s5_params(lam_re, lam_im, b_re, b_im, c_re, c_im, d, log_dt):
    g, p = lam_re.shape
    gs = b_re.shape[2]
    gpb = LANES // gs
    gb = g // gpb
    dt = jnp.exp(log_dt.astype(F32))[:, None]
    mag = jnp.exp(lam_re * dt)
    lb_re = mag * jnp.cos(lam_im * dt)
    lb_im = mag * jnp.sin(lam_im * dt)
    den = lam_re * lam_re + lam_im * lam_im
    q_re = ((lb_re - 1.0) * lam_re + lb_im * lam_im) / den
    q_im = (lb_im * lam_re - (lb_re - 1.0) * lam_im) / den
    bb_re = q_re[..., None] * b_re - q_im[..., None] * b_im
    bb_im = q_re[..., None] * b_im + q_im[..., None] * b_re
    eye = jnp.eye(gpb, dtype=F32)

    def pack_in(b):
        b = b.reshape(gb, gpb, p, gs).transpose(0, 1, 3, 2)
        return jnp.einsum('ab,xahp->xahbp', eye, b).reshape(gb, gpb * gs, gpb * p).astype(BF16)

    def pack_out(c):
        c = c.reshape(gb, gpb, gs, p).transpose(0, 1, 3, 2)
        return jnp.einsum('ab,xaph->xapbh', eye, c).reshape(gb, gpb * p, gpb * gs).astype(BF16)

    return dict(
        bre=pack_in(bb_re), bim=pack_in(bb_im), cre=pack_out(c_re.astype(F32)), cim=pack_out(c_im.astype(F32)),
        lre=lb_re.reshape(gb, 1, gpb * p), lim=lb_im.reshape(gb, 1, gpb * p),
        d=d.astype(F32).reshape(gb, 1, gpb * gs), gb=gb, s=gpb * p)


def _s5_scan(proj, row0, bsz, seq, prm, h0_re, h0_im):
    gb, s = prm['gb'], prm['s']
    t_rows = _tile(seq, S5_T, 8)
    n_t = seq // t_rows
    blk0 = row0 // t_rows
    assert row0 % t_rows == 0
    h0_re = h0_re.astype(F32).reshape(bsz * gb, 1, s)
    h0_im = h0_im.astype(F32).reshape(bsz * gb, 1, s)
    par = lambda shape: pl.BlockSpec((1,) + shape, lambda b, g, i: (g, 0, 0))
    st = pl.BlockSpec((1, 1, s), lambda b, g, i: (b * gb + g, 0, 0))
    zspec = pl.BlockSpec((t_rows, LANES), lambda b, g, i: (b * n_t + i, g))
    z, zbf, hre, him = pl.pallas_call(
        functools.partial(_s5_body, t_rows=t_rows, n_t=n_t),
        name="s5_scan",
        grid=(bsz, gb, n_t),
        in_specs=[pl.BlockSpec((t_rows, LANES), lambda b, g, i: (blk0 + b * n_t + i, g)),
                  par((LANES, s)), par((LANES, s)), par((s, LANES)), par((s, LANES)),
                  par((1, s)), par((1, s)), par((1, LANES)), st, st],
        out_specs=[zspec, zspec, st, st],
        out_shape=[jax.ShapeDtypeStruct((bsz * seq, gb * LANES), F32),
                   jax.ShapeDtypeStruct((bsz * seq, gb * LANES), BF16),
                   jax.ShapeDtypeStruct((bsz * gb, 1, s), F32),
                   jax.ShapeDtypeStruct((bsz * gb, 1, s), F32)],
        scratch_shapes=[pltpu.VMEM((1, s), F32), pltpu.VMEM((1, s), F32)],
        compiler_params=_cparams("parallel", "parallel", "arbitrary"),
    )(proj, prm['bre'], prm['bim'], prm['cre'], prm['cim'], prm['lre'], prm['lim'], prm['d'], h0_re, h0_im)
    return z, zbf, hre, him


def _softplus(x):
    return jnp.maximum(x, 0.0) + jnp.log1p(jnp.exp(-jnp.abs(x)))


def _rwkv_prep_body(pr_ref, pk_ref, pv_ref, pl_ref, sr_ref, sk_ref, sv_ref, sl_ref,
                    mr_ref, mk_ref, mv_ref, ml_ref, w0_ref, a0_ref, w2_ref, a2_ref, g2_ref,
                    r_ref, k_ref, v_ref, w_ref, a_ref, g_ref, cr, ck, cv, cl):
    i = pl.program_id(1)

    @pl.when(i == 0)
    def _():
        cr[...] = sr_ref[0]
        ck[...] = sk_ref[0]
        cv[...] = sv_ref[0]
        cl[...] = sl_ref[0]

    def mixed(p_ref, carry, mu_ref):
        x = p_ref[...]
        row = lax.broadcasted_iota(jnp.int32, x.shape, 0)
        prev = jnp.where(row == 0, carry[...], pltpu.roll(x, 1, 0))
        carry[...] = x[x.shape[0] - 1:, :]
        return x + (prev - x) * mu_ref[...]

    r_ref[...] = mixed(pr_ref, cr, mr_ref)
    k_ref[...] = mixed(pk_ref, ck, mk_ref)
    v_ref[...] = mixed(pv_ref, cv, mv_ref)
    lo = mixed(pl_ref, cl, ml_ref)
    tw = jnp.dot(jnp.tanh(lo).astype(BF16), w2_ref[...], preferred_element_type=F32)
    w_log = -_softplus(-(w0_ref[...] + tw)) - 0.5
    w_ref[...] = jnp.exp(-jnp.exp(w_log))
    a_ref[...] = jax.nn.sigmoid(a0_ref[...] + jnp.dot(lo.astype(BF16), a2_ref[...], preferred_element_type=F32))
    g_ref[...] = jnp.dot(jax.nn.sigmoid(lo).astype(BF16), g2_ref[...], preferred_element_type=F32)


def _rwkv_prep(proj, row0, bsz, seq, c, lp, col_r, shift, prm):
    t_rows = _tile(seq, RWKV_PREP_T, 8)
    n_t = seq // t_rows
    assert row0 % t_rows == 0 and col_r % c == 0 and (col_r + 3 * c) % lp == 0
    blk0 = row0 // t_rows
    cb = col_r // c
    lb = (col_r + 3 * c) // lp
    pin = lambda j: pl.BlockSpec((t_rows, c), lambda b, i: (blk0 + b * n_t + i, cb + j))
    sh = lambda w: pl.BlockSpec((1, 1, w), lambda b, i: (b, 0, 0))
    vec = lambda w: pl.BlockSpec((1, w), lambda b, i: (0, 0))
    mat = pl.BlockSpec((lp, c), lambda b, i: (0, 0))
    out = pl.BlockSpec((t_rows, c), lambda b, i: (b * n_t + i, 0))
    sds = jax.ShapeDtypeStruct((bsz * seq, c), F32)
    return pl.pallas_call(
        _rwkv_prep_body,
        name="rwkv_prep",
        grid=(bsz, n_t),
        in_specs=[pin(0), pin(1), pin(2),
                  pl.BlockSpec((t_rows, lp), lambda b, i: (blk0 + b * n_t + i, lb)),
                  sh(c), sh(c), sh(c), sh(lp), vec(c), vec(c), vec(c), vec(lp), vec(c), vec(c), mat, mat, mat],
        out_specs=[out] * 6,
        out_shape=[sds] * 6,
        scratch_shapes=[pltpu.VMEM((1, c), F32)] * 3 + [pltpu.VMEM((1, lp), F32)],
        compiler_params=_cparams("parallel", "arbitrary"),
    )(proj, proj, proj, proj, shift[0], shift[1], shift[2], shift[3],
      prm['mu_r'], prm['mu_k'], prm['mu_v'], prm['mu_l'], prm['w0'], prm['a0'], prm['w2'], prm['a2'], prm['g2'])


def _rwkv_scan_body(r_ref, k_ref, v_ref, w_ref, a_ref, kkp_ref, kap_ref, rkp_ref, lnw_ref, lnb_ref, s0_ref,
                    y_ref, sout_ref, st, kk_s, kka_s, km_s, wr_s, c1_s, c2_s, yr_s, *, t_rows, n_t, hd):
    i = pl.program_id(1)

    @pl.when(i == 0)
    def _():
        st[...] = s0_ref[...]

    r = r_ref[...]
    kraw = k_ref[...]
    a = a_ref[...]
    kk = kraw * kkp_ref[...][None]
    kk = kk * lax.rsqrt(jnp.maximum(jnp.sum(kk * kk, axis=1, keepdims=True), 1e-24))
    km = kraw * (1.0 + (a - 1.0) * kap_ref[...][None])
    kka = kk * a
    kk_s[...] = kk
    kka_s[...] = kka
    km_s[...] = km
    wr_s[...] = w_ref[...] * r
    c1_s[...] = jnp.sum(kka * r, axis=1, keepdims=True)
    c2_s[...] = jnp.sum(km * r, axis=1, keepdims=True)

    def step(t, carry):
        sk = jnp.zeros((hd, LANES), F32)
        yq = jnp.zeros((hd, LANES), F32)
        for j in range(hd):
            sj = st[j]
            sk = sk + sj * kk_s[t, pl.ds(j, 1), :]
            yq = yq + sj * wr_s[t, pl.ds(j, 1), :]
        v_t = v_ref[t]
        for j in range(hd):
            st[j] = (st[j] * w_ref[t, pl.ds(j, 1), :] - kka_s[t, pl.ds(j, 1), :] * sk
                     + km_s[t, pl.ds(j, 1), :] * v_t)
        yr_s[t] = yq - sk * c1_s[t] + v_t * c2_s[t]
        return carry

    lax.fori_loop(0, t_rows, step, 0)
    y = yr_s[...]
    ym = jnp.mean(y, axis=1, keepdims=True)
    yc = y - ym
    yv = jnp.mean(yc * yc, axis=1, keepdims=True)
    yn = yc * lax.rsqrt(yv + RWKV_GN_EPS) * lnw_ref[...][None] + lnb_ref[...][None]
    bonus = jnp.sum(r * km * rkp_ref[...][None], axis=1, keepdims=True) * v_ref[...]
    y_ref[...] = yn + bonus

    @pl.when(i == n_t - 1)
    def _():
        sout_ref[...] = st[...]


def _rwkv_scan(r, k, v, w, a, prm, s0):
    seq, hd, bh = r.shape
    t_rows = _tile(seq, RWKV_SCAN_T, 1)
    n_t = seq // t_rows
    n_lb = bh // LANES
    seq_spec = pl.BlockSpec((t_rows, hd, LANES), lambda l, i: (i, 0, l))
    par = pl.BlockSpec((hd, LANES), lambda l, i: (0, l))
    st_spec = pl.BlockSpec((hd, hd, LANES), lambda l, i: (0, 0, l))
    big = pltpu.VMEM((t_rows, hd, LANES), F32)
    small = pltpu.VMEM((t_rows, 1, LANES), F32)
    return pl.pallas_call(
        functools.partial(_rwkv_scan_body, t_rows=t_rows, n_t=n_t, hd=hd),
        name="rwkv_scan",
        grid=(n_lb, n_t),
        in_specs=[seq_spec] * 5 + [par] * 5 + [st_spec],
        out_specs=[seq_spec, st_spec],
        out_shape=[jax.ShapeDtypeStruct((seq, hd, bh), F32), jax.ShapeDtypeStruct((hd, hd, bh), F32)],
        scratch_shapes=[pltpu.VMEM((hd, hd, LANES), F32), big, big, big, big, small, small, big],
        compiler_params=_cparams("parallel", "arbitrary"),
    )(r, k, v, w, a, prm['k_k'], prm['k_a'], prm['r_k'], prm['ln_w'], prm['ln_b'], s0)


def _mul_cast_body(y_ref, g_ref, o_ref):
    o_ref[...] = (y_ref[...] * g_ref[...]).astype(o_ref.dtype)


def _mul_cast(y, g):
    n, d = y.shape
    tm = _tile(n, LN_TM * 2, 16)
    spec = pl.BlockSpec((tm, d), lambda i: (i, 0))
    return pl.pallas_call(
        _mul_cast_body, name="mul_cast", grid=(n // tm,), in_specs=[spec, spec], out_specs=spec,
        out_shape=jax.ShapeDtypeStruct((n, d), BF16), compiler_params=_cparams("parallel"),
    )(y, g)


def _rope_lanes(x, cos, sin, half):
    lane = lax.broadcasted_iota(jnp.int32, x.shape, 1)
    swapped = jnp.where(lane < half, pltpu.roll(x, LANES - half, 1), pltpu.roll(x, half, 1))
    return x * cos + swapped * sin


def _mla_prep_body(p_ref, qg_ref, kg_ref, cos_ref, sin_ref, cq_ref, cqt_ref, ckv_ref, ckr_ref, ckvt_ref, kr_ref,
                   *, ql, kl, half):
    x = p_ref[...]
    cq = x[:, :ql]
    cq = cq * lax.rsqrt(jnp.mean(cq * cq, axis=-1, keepdims=True) + RMS_EPS) * qg_ref[...]
    cq_ref[...] = cq.astype(BF16)
    cqt_ref[...] = cq.T.astype(BF16)
    ckv = x[:, ql:ql + kl]
    ckv = ckv * lax.rsqrt(jnp.mean(ckv * ckv, axis=-1, keepdims=True) + RMS_EPS) * kg_ref[...]
    ckv_ref[...] = ckv
    ckvt_ref[...] = ckv.T.astype(BF16)
    kr = _rope_lanes(x[:, ql + kl:], cos_ref[...], sin_ref[...], half)
    kr_ref[...] = kr
    ckr_ref[...] = jnp.concatenate([ckv, kr], axis=1).astype(BF16)


def _mla_prep(proj, q_norm, kv_norm, cos, sin, ql, kl, half):
    n, width = proj.shape
    tm = _tile(n, LN_TM, LANES)
    row = lambda w: pl.BlockSpec((tm, w), lambda i: (i, 0))
    col = lambda w: pl.BlockSpec((w, tm), lambda i: (0, i))
    vec = lambda w: pl.BlockSpec((1, w), lambda i: (0, 0))
    return pl.pallas_call(
        functools.partial(_mla_prep_body, ql=ql, kl=kl, half=half),
        name="mla_prep",
        grid=(n // tm,),
        in_specs=[row(width), vec(ql), vec(kl), row(LANES), row(LANES)],
        out_specs=[row(ql), col(ql), row(kl), row(kl + LANES), col(kl), row(LANES)],
        out_shape=[jax.ShapeDtypeStruct((n, ql), BF16), jax.ShapeDtypeStruct((ql, n), BF16),
                   jax.ShapeDtypeStruct((n, kl), F32), jax.ShapeDtypeStruct((n, kl + LANES), BF16),
                   jax.ShapeDtypeStruct((kl, n), BF16), jax.ShapeDtypeStruct((n, LANES), F32)],
        compiler_params=_cparams("parallel"),
    )(proj, q_norm.reshape(1, ql), kv_norm.reshape(1, kl), cos, sin)


def _online_softmax_update(s, v_blk, m_s, l_s, acc_s):
    m_prev = m_s[...]
    m_new = jnp.maximum(m_prev, jnp.max(s, axis=1, keepdims=True))
    alpha = jnp.exp(m_prev - m_new)
    p = jnp.exp(s - m_new)
    l_s[...] = alpha * l_s[...] + jnp.sum(p, axis=1, keepdims=True)
    acc_s[...] = alpha * acc_s[...] + jnp.dot(p.astype(BF16), v_blk, preferred_element_type=F32)
    m_s[...] = m_new


_NT = (((1,), (1,)), ((), ()))


def _attn_prompt_body(qt_ref, k_ref, vt_ref, o_ref, m_s, l_s, acc_s, *, tq, c):
    i = pl.program_id(2)
    m_s[...] = jnp.full(m_s.shape, MASK_VALUE, F32)
    l_s[...] = jnp.zeros(l_s.shape, F32)
    acc_s[...] = jnp.zeros(acc_s.shape, F32)
    qt = qt_ref[...]

    def update(j, masked):
        off = pl.multiple_of(j * tq, tq)
        s = jnp.dot(k_ref[pl.ds(off, tq), :], qt, preferred_element_type=F32)
        if masked:
            key = lax.broadcasted_iota(jnp.int32, s.shape, 0)
            qry = lax.broadcasted_iota(jnp.int32, s.shape, 1)
            s = jnp.where(key // CHUNK <= qry // CHUNK, s, MASK_VALUE)
        m_prev = m_s[...]
        m_new = jnp.maximum(m_prev, jnp.max(s, axis=0, keepdims=True))
        alpha = jnp.exp2((m_prev - m_new) * c)
        p = jnp.exp2((s - m_new) * c)
        l_s[...] = alpha * l_s[...] + jnp.sum(p, axis=0, keepdims=True)
        acc_s[...] = alpha * acc_s[...] + jnp.dot(vt_ref[:, pl.ds(off, tq)], p.astype(BF16),
                                                  preferred_element_type=F32)
        m_s[...] = m_new

    def full_tile(j, carry):
        update(j, False)
        return carry

    lax.fori_loop(0, i, full_tile, 0)
    update(i, True)
    o_ref[...] = (acc_s[...] / l_s[...]).T.astype(o_ref.dtype)


def _attn_prompt(qt, k, vt, bsz, seq, heads, scale):
    dk = qt.shape[0] // heads
    tq = _tile(seq, ATTN_TQ, LANES)
    assert tq % CHUNK == 0
    n_q = seq // tq
    return pl.pallas_call(
        functools.partial(_attn_prompt_body, tq=tq, c=scale * math.log2(math.e)),
        name="attn_prompt",
        grid=(bsz, heads, n_q),
        in_specs=[pl.BlockSpec((dk, tq), lambda b, h, i: (h, b * n_q + i)),
                  pl.BlockSpec((seq, dk), lambda b, h, i: (b, h)),
                  pl.BlockSpec((LANES, seq), lambda b, h, i: (h, b))],
        out_specs=pl.BlockSpec((tq, LANES), lambda b, h, i: (b * n_q + i, h)),
        out_shape=jax.ShapeDtypeStruct((bsz * seq, heads * LANES), BF16),
        scratch_shapes=[pltpu.VMEM((1, tq), F32), pltpu.VMEM((1, tq), F32), pltpu.VMEM((LANES, tq), F32)],
        compiler_params=_cparams("parallel", "parallel", "arbitrary"),
    )(qt, k, vt)


def _attn_latent_body(ql_ref, qr_ref, kv_ref, kr_ref, o_ref, m_s, l_s, acc_s,
                      *, tq, tk, n_k, heads, q_pos0, n_keys, scale):
    i = pl.program_id(1)
    m_s[...] = jnp.full(m_s.shape, MASK_VALUE, F32)
    l_s[...] = jnp.zeros(l_s.shape, F32)
    acc_s[...] = jnp.zeros(acc_s.shape, F32)
    ql = ql_ref[...]
    qr = qr_ref[...]

    def tile(j, carry):
        off = pl.multiple_of(j * tk, tk)
        kv = kv_ref[pl.ds(off, tk), :]
        s = lax.dot_general(ql, kv, _NT, preferred_element_type=F32)
        s = s + lax.dot_general(qr, kr_ref[pl.ds(off, tk), :], _NT, preferred_element_type=F32)
        row = lax.broadcasted_iota(jnp.int32, s.shape, 0)
        col = lax.broadcasted_iota(jnp.int32, s.shape, 1)
        q_pos = q_pos0 + (i * tq + row) // heads
        k_pos = off + col
        visible = (k_pos // CHUNK <= q_pos // CHUNK) & (k_pos < n_keys)
        s = jnp.where(visible, s * scale, MASK_VALUE)
        _online_softmax_update(s, kv, m_s, l_s, acc_s)
        return carry

    lax.fori_loop(0, n_k, tile, 0)
    o_ref[...] = (acc_s[...] / l_s[...]).astype(o_ref.dtype)


def _attn_latent(q_lat, q_rope, kv_all, kr_all, bsz, rows, heads, q_pos0, n_keys, scale):
    kl = q_lat.shape[1]
    nkp = kv_all.shape[0] // bsz
    tq = _tile(rows, LATENT_T, 16)
    tk = _tile(nkp, LATENT_T, 16)
    n_q = rows // tq
    return pl.pallas_call(
        functools.partial(_attn_latent_body, tq=tq, tk=tk, n_k=nkp // tk, heads=heads,
                          q_pos0=q_pos0, n_keys=n_keys, scale=scale),
        name="attn_latent",
        grid=(bsz, n_q),
        in_specs=[pl.BlockSpec((tq, kl), lambda b, i: (b * n_q + i, 0)),
                  pl.BlockSpec((tq, LANES), lambda b, i: (b * n_q + i, 0)),
                  pl.BlockSpec((nkp, kl), lambda b, i: (b, 0)),
                  pl.BlockSpec((nkp, LANES), lambda b, i: (b, 0))],
        out_specs=pl.BlockSpec((tq, kl), lambda b, i: (b * n_q + i, 0)),
        out_shape=jax.ShapeDtypeStruct((bsz * rows, kl), BF16),
        scratch_shapes=[pltpu.VMEM((tq, 1), F32), pltpu.VMEM((tq, 1), F32), pltpu.VMEM((tq, kl), F32)],
        compiler_params=_cparams("parallel", "arbitrary"),
    )(q_lat, q_rope, kv_all, kr_all)


def _router_body(x_ref, wh_ref, wl_ref, o_ref):
    x = x_ref[...]
    xh = x.astype(BF16)
    xl = (x - xh.astype(F32)).astype(BF16)
    wh = wh_ref[...]
    acc = jnp.dot(xh, wh, preferred_element_type=F32)
    acc = acc + jnp.dot(xl, wh, preferred_element_type=F32)
    acc = acc + jnp.dot(xh, wl_ref[...], preferred_element_type=F32)
    o_ref[...] = jax.nn.sigmoid(acc)


def _router(x, w):
    n, d = x.shape
    e = w.shape[1]
    tm = _tile(n, ROUTER_TM, 8)
    wh = w.astype(BF16)
    wl = (w - wh.astype(F32)).astype(BF16)
    wspec = pl.BlockSpec((d, e), lambda i: (0, 0))
    return pl.pallas_call(
        _router_body, name="moe_router", grid=(n // tm,),
        in_specs=[pl.BlockSpec((tm, d), lambda i: (i, 0)), wspec, wspec],
        out_specs=pl.BlockSpec((tm, e), lambda i: (i, 0)),
        out_shape=jax.ShapeDtypeStruct((n, e), F32),
        compiler_params=_cparams("parallel"),
    )(x, wh, wl)


def _route_body(s_ref, b_ref, idx_ref, w_ref, sel_ref):
    s = s_ref[...]
    t, e = s.shape
    gsz = e // N_GROUP
    biased = s + b_ref[...]
    lane = lax.broadcasted_iota(jnp.int32, (t, e), 1)
    grp = lane // gsz
    neg = -jnp.inf
    gscore = []
    for g in range(N_GROUP):
        v = jnp.where(grp == g, biased, neg)
        m1 = jnp.max(v, axis=1, keepdims=True)
        first = jnp.min(jnp.where(v == m1, lane, e), axis=1, keepdims=True)
        m2 = jnp.max(jnp.where(lane == first, neg, v), axis=1, keepdims=True)
        gscore.append(m1 + m2)
    taken = [jnp.zeros((t, 1), jnp.bool_) for _ in range(N_GROUP)]
    for _ in range(TOPK_GROUP):
        avail = [jnp.where(taken[g], neg, gscore[g]) for g in range(N_GROUP)]
        best = avail[0]
        for g in range(1, N_GROUP):
            best = jnp.maximum(best, avail[g])
        found = jnp.zeros((t, 1), jnp.bool_)
        for g in range(N_GROUP):
            pick = jnp.logical_and(jnp.logical_not(found), avail[g] == best)
            pick = jnp.logical_and(pick, jnp.logical_not(taken[g]))
            taken[g] = jnp.logical_or(taken[g], pick)
            found = jnp.logical_or(found, pick)
    keep = jnp.zeros((t, e), jnp.bool_)
    for g in range(N_GROUP):
        keep = jnp.logical_or(keep, jnp.logical_and(grp == g, taken[g]))
    cand = jnp.where(keep, biased, neg)
    idx_out = jnp.zeros((t, e), jnp.int32)
    w_out = jnp.zeros((t, e), F32)
    sel = jnp.zeros((t, e), jnp.bool_)
    w_sum = jnp.zeros((t, 1), F32)
    for k in range(TOP_K):
        m = jnp.max(cand, axis=1, keepdims=True)
        pos = jnp.min(jnp.where(cand == m, lane, e), axis=1, keepdims=True)
        hit = lane == pos
        w_k = jnp.sum(jnp.where(hit, s, 0.0), axis=1, keepdims=True)
        cand = jnp.where(hit, neg, cand)
        sel = jnp.logical_or(sel, hit)
        idx_out = jnp.where(lane == k, pos, idx_out)
        w_out = jnp.where(lane == k, w_k, w_out)
        w_sum = w_sum + w_k
    idx_ref[...] = idx_out
    w_ref[...] = w_out / (w_sum + 1e-20) * ROUTED_SCALE
    sel_ref[...] = jnp.where(sel, 1.0, 0.0).astype(BF16)


def _route(scores, bias):
    n, e = scores.shape
    t = _tile(n, ROUTER_TM, 16)
    spec = pl.BlockSpec((t, e), lambda i: (i, 0))
    return pl.pallas_call(
        _route_body, name="moe_route", grid=(n // t,),
        in_specs=[spec, pl.BlockSpec((1, e), lambda i: (0, 0))],
        out_specs=[spec, spec, spec],
        out_shape=[jax.ShapeDtypeStruct((n, e), jnp.int32), jax.ShapeDtypeStruct((n, e), F32),
                   jax.ShapeDtypeStruct((n, e), BF16)],
        compiler_params=_cparams("parallel"),
    )(scores, bias.astype(F32).reshape(1, e))


def _rank_body(sel_ref, cum_ref, cnt_ref, carry):
    @pl.when(pl.program_id(0) == 0)
    def _():
        carry[...] = jnp.zeros(carry.shape, F32)

    sel = sel_ref[...]
    t = sel.shape[0]
    earlier = (lax.broadcasted_iota(jnp.int32, (t, t), 1) < lax.broadcasted_iota(jnp.int32, (t, t), 0))
    cum = jnp.dot(earlier.astype(BF16), sel, preferred_element_type=F32) + carry[...]
    cum_ref[...] = cum.astype(jnp.int32)
    carry[...] = carry[...] + jnp.sum(sel.astype(F32), axis=0, keepdims=True)
    cnt_ref[...] = carry[...].astype(jnp.int32)


def _rank(sel):
    n, e = sel.shape
    t = _tile(n, ROUTER_TM, 16)
    return pl.pallas_call(
        _rank_body, name="moe_rank", grid=(n // t,),
        in_specs=[pl.BlockSpec((t, e), lambda i: (i, 0))],
        out_specs=[pl.BlockSpec((t, e), lambda i: (i, 0)), pl.BlockSpec((1, e), lambda i: (0, 0))],
        out_shape=[jax.ShapeDtypeStruct((n, e), jnp.int32), jax.ShapeDtypeStruct((1, e), jnp.int32)],
        scratch_shapes=[pltpu.VMEM((1, e), F32)],
        compiler_params=_cparams("arbitrary"),
    )(sel)


def _moe_block(b, be_ref, nu_ref):
    bb = jnp.minimum(b, nu_ref[0] - 1)
    changed = (b == 0) | (be_ref[bb] != be_ref[jnp.maximum(bb - 1, 0)])
    return b < nu_ref[0], changed


def _moe_up_body(be_ref, nu_ref, x_ref, wg_ref, wu_ref, o_ref, wgb, wub):
    active, changed = _moe_block(pl.program_id(1), be_ref, nu_ref)

    @pl.when(active & changed)
    def _():
        wgb[...] = wg_ref[0, 0].astype(BF16)
        wub[...] = wu_ref[0, 0].astype(BF16)

    @pl.when(active)
    def _():
        x = x_ref[...].astype(BF16)
        h1 = jnp.dot(x, wgb[...], preferred_element_type=F32)
        h2 = jnp.dot(x, wub[...], preferred_element_type=F32)
        o_ref[...] = (h1 * jax.nn.sigmoid(h1) * h2).astype(o_ref.dtype)


def _moe_down_body(be_ref, nu_ref, h_ref, wd_ref, o_ref, wdb):
    active, changed = _moe_block(pl.program_id(1), be_ref, nu_ref)

    @pl.when(active & changed)
    def _():
        wdb[...] = wd_ref[0, 0].astype(BF16)

    @pl.when(active)
    def _():
        o_ref[...] = jnp.dot(h_ref[...], wdb[...], preferred_element_type=F32)


def _moe_down_split_body(be_ref, nu_ref, h_ref, wd_ref, hi_ref, lo_ref, wdb):
    active, changed = _moe_block(pl.program_id(1), be_ref, nu_ref)

    @pl.when(active & changed)
    def _():
        wdb[...] = wd_ref[0, 0].astype(BF16)

    @pl.when(active)
    def _():
        y = jnp.dot(h_ref[...], wdb[...], preferred_element_type=F32)
        hi = y.astype(BF16)
        hi_ref[...] = hi
        lo_ref[...] = (y - hi.astype(F32)).astype(BF16)


def _expert_ffn(xs, blk_expert, n_used, w_gate, w_up, w_down, layer, tm, split=False):
    p, d = xs.shape
    ff = w_gate.shape[3]
    n_blk = p // tm
    tf = _tile(ff, MOE_TF, LANES)
    tn = _tile(d, MOE_TN, LANES)
    blk = lambda b, nu: jnp.minimum(b, nu[0] - 1)
    hid = pl.pallas_call(
        _moe_up_body, name="moe_up",
        grid_spec=pltpu.PrefetchScalarGridSpec(
            num_scalar_prefetch=2, grid=(ff // tf, n_blk),
            in_specs=[pl.BlockSpec((tm, d), lambda f, b, be, nu: (blk(b, nu), 0)),
                      pl.BlockSpec((1, 1, d, tf), lambda f, b, be, nu: (layer, be[blk(b, nu)], 0, f)),
                      pl.BlockSpec((1, 1, d, tf), lambda f, b, be, nu: (layer, be[blk(b, nu)], 0, f))],
            out_specs=pl.BlockSpec((tm, tf), lambda f, b, be, nu: (blk(b, nu), f)),
            scratch_shapes=[pltpu.VMEM((d, tf), BF16), pltpu.VMEM((d, tf), BF16)]),
        out_shape=jax.ShapeDtypeStruct((p, ff), BF16),
        compiler_params=_cparams("arbitrary", "arbitrary"),
    )(blk_expert, n_used, xs, w_gate, w_up)
    out_spec = pl.BlockSpec((tm, tn), lambda n, b, be, nu: (blk(b, nu), n))
    return pl.pallas_call(
        _moe_down_split_body if split else _moe_down_body, name="moe_down",
        grid_spec=pltpu.PrefetchScalarGridSpec(
            num_scalar_prefetch=2, grid=(d // tn, n_blk),
            in_specs=[pl.BlockSpec((tm, ff), lambda n, b, be, nu: (blk(b, nu), 0)),
                      pl.BlockSpec((1, 1, ff, tn), lambda n, b, be, nu: (layer, be[blk(b, nu)], 0, n))],
            out_specs=[out_spec, out_spec] if split else out_spec,
            scratch_shapes=[pltpu.VMEM((ff, tn), BF16)]),
        out_shape=[jax.ShapeDtypeStruct((p, d), BF16)] * 2 if split else jax.ShapeDtypeStruct((p, d), F32),
        compiler_params=_cparams("arbitrary", "arbitrary"),
    )(blk_expert, n_used, hid, w_down)


def _moe_ffn(x, x_bf, layer, w_router, b_router, w_gate, w_up, w_down, ws_gate, ws_up, ws_down):
    n_tok, d = x.shape
    n_exp = w_router.shape[2]
    scores = _router(x, w_router[layer].astype(F32))
    top_idx, top_w, sel = _route(scores, b_router[layer])
    top_idx, top_w = top_idx[:, :TOP_K], top_w[:, :TOP_K]
    n_asg = n_tok * TOP_K
    tm = _tile(n_tok, MOE_TM, 16)
    n_blk = -(-n_asg // tm) + n_exp
    before, counts = _rank(sel)
    counts = counts[0]
    padded = (counts + tm - 1) // tm * tm
    pad_end = jnp.cumsum(padded)
    pad_start = pad_end - padded
    dest = (pad_start[top_idx] + jnp.take_along_axis(before, top_idx, axis=-1)).astype(jnp.int32)
    tok = jnp.broadcast_to(jnp.arange(n_tok, dtype=jnp.int32)[:, None], dest.shape)
    slot_tok = jnp.zeros((n_blk * tm,), jnp.int32).at[dest.reshape(-1)].set(tok.reshape(-1))
    blk_expert = jnp.minimum(jnp.searchsorted(pad_end, jnp.arange(n_blk) * tm, side='right'),
                             n_exp - 1).astype(jnp.int32)
    n_used = (pad_end[-1] // tm).astype(jnp.int32).reshape(1)
    xs = x[slot_tok]
    y_slots = _expert_ffn(xs, blk_expert, n_used, w_gate, w_up, w_down, layer, tm)
    routed = (y_slots[dest] * top_w[:, :, None]).sum(axis=1)
    n_sblk = n_tok // tm
    shared = _expert_ffn(x_bf, jnp.zeros((n_sblk,), jnp.int32), jnp.full((1,), n_sblk, jnp.int32),
                         ws_gate[:, None], ws_up[:, None], ws_down[:, None], layer, tm)
    return routed, shared


def _to_scan_layout(t, bsz, seq, heads, hd, bh_pad):
    t = t.reshape(bsz, seq, heads, hd).transpose(1, 3, 0, 2).reshape(seq, hd, bsz * heads)
    return jnp.pad(t, ((0, 0), (0, 0), (0, bh_pad - bsz * heads)))


def _rope_tables(pos, half):
    inv = jnp.exp(-math.log(ROPE_THETA) * jnp.arange(half, dtype=F32) / half)
    ang = pos.astype(F32)[:, None] * inv[None, :]
    cos, sin = jnp.cos(ang), jnp.sin(ang)
    pad = LANES - 2 * half
    ones = jnp.ones((pos.shape[0], pad), F32)
    zeros = jnp.zeros((pos.shape[0], pad), F32)
    return jnp.concatenate([cos, cos, ones], 1), jnp.concatenate([-sin, sin, zeros], 1)


def kernel(x_prompt, x_sample, state_s5_re, state_s5_im, state_rwkv, state_rwkv_shift, cache_c_kv, cache_k_rope, l0_w_in, s5_lambda_re, s5_lambda_im, s5_b_re, s5_b_im, s5_c_re, s5_c_im, s5_d, s5_log_dt, s5_w_glu, s5_b_glu, rwkv_mu, rwkv_w0, rwkv_w2, rwkv_a0, rwkv_a2, rwkv_g2, rwkv_k_k, rwkv_k_a, rwkv_r_k, rwkv_ln_w, rwkv_ln_b, l0_w_out, l1_w_in, mla_q_norm, mla_w_uq, mla_kv_norm, mla_w_uk, mla_w_uv, l1_w_out, ln_mix_g, ln_mix_b, ln_ffn_g, ln_ffn_b, moe_w_router, moe_b_router, moe_w_gate, moe_w_up, moe_w_down, moe_ws_gate, moe_ws_up, moe_ws_down):
    bp, lp_, d = x_prompt.shape
    bs, ls, _ = x_sample.shape
    n_p, n_s = bp * lp_, bs * ls
    depth = ln_mix_g.shape[0]
    alpha = (2 * depth) ** 0.25
    groups = (("p", bp, lp_, 0), ("s", bs, ls, n_p))

    x = jnp.concatenate([x_prompt.reshape(n_p, d), x_sample.reshape(n_s, d)], axis=0).astype(F32)
    x_bf = x.astype(BF16)
    moe_p = (moe_w_router, moe_b_router, moe_w_gate, moe_w_up, moe_w_down, moe_ws_gate, moe_ws_up, moe_ws_down)

    s5_g, s5_p = s5_lambda_re.shape
    ws = s5_g * s5_b_re.shape[2]
    heads_r, hd = rwkv_r_k.shape
    c = heads_r * hd
    lora_w, lora_a, lora_g = rwkv_w2.shape[0], rwkv_a2.shape[0], rwkv_g2.shape[0]
    lora = lora_w + lora_a + lora_g
    lp = -(-lora // LANES) * LANES
    proj_w = ws + 3 * c + lora
    w_in = jnp.pad(l0_w_in, ((0, 0), (0, lp - lora))).astype(BF16)
    proj = _mm(x_bf, w_in, tn=512, name="mm_l0_in")
    s5p = _s5_params(s5_lambda_re.astype(F32), s5_lambda_im.astype(F32), s5_b_re.astype(F32),
                     s5_b_im.astype(F32), s5_c_re, s5_c_im, s5_d, s5_log_dt)
    mu = jnp.pad(rwkv_mu.astype(F32), (0, lp - lora)).reshape(1, -1)
    row_pad = lambda m, at: jnp.pad(m.astype(F32), ((at, lp - at - m.shape[0]), (0, 0))).astype(BF16)
    prep_p = dict(mu_r=mu[:, :c], mu_k=mu[:, c:2 * c], mu_v=mu[:, 2 * c:3 * c], mu_l=mu[:, 3 * c:],
                  w0=rwkv_w0.astype(F32).reshape(1, c), a0=rwkv_a0.astype(F32).reshape(1, c),
                  w2=row_pad(rwkv_w2, 0), a2=row_pad(rwkv_a2, lora_w), g2=row_pad(rwkv_g2, lora_w + lora_a))

    zs, zbfs, yrs, outs0 = [], [], [], {}
    for name, bsz, seq, row0 in groups:
        if name == "p":
            h0re = jnp.zeros((bsz, s5_g, s5_p), F32)
            h0im = h0re
            s0 = jnp.zeros((bsz, heads_r, hd, hd), F32)
            shift0 = jnp.zeros((bsz, proj_w - ws), F32)
        else:
            h0re, h0im, s0, shift0 = state_s5_re, state_s5_im, state_rwkv.astype(F32), state_rwkv_shift.astype(F32)
        z, zbf, hre, him = _s5_scan(proj, row0, bsz, seq, s5p, h0re, h0im)
        zs.append(z)
        zbfs.append(zbf)
        shift0 = jnp.pad(shift0, ((0, 0), (0, lp - lora)))[:, None, :]
        shifts = (shift0[..., :c], shift0[..., c:2 * c], shift0[..., 2 * c:3 * c], shift0[..., 3 * c:])
        r, k, v, w, a, g = _rwkv_prep(proj, row0, bsz, seq, c, lp, ws, shifts, prep_p)
        bh = bsz * heads_r
        bh_pad = -(-bh // LANES) * LANES
        lay = lambda t: _to_scan_layout(t, bsz, seq, heads_r, hd, bh_pad)
        par = lambda t: jnp.pad(jnp.tile(t.astype(F32).reshape(heads_r, hd).T, (1, bsz)), ((0, 0), (0, bh_pad - bh)))
        scan_p = dict(k_k=par(rwkv_k_k), k_a=par(rwkv_k_a), r_k=par(rwkv_r_k), ln_w=par(rwkv_ln_w), ln_b=par(rwkv_ln_b))
        s0l = jnp.pad(s0.transpose(3, 2, 0, 1).reshape(hd, hd, bh), ((0, 0), (0, 0), (0, bh_pad - bh)))
        y_scan, s_last = _rwkv_scan(lay(r), lay(k), lay(v), lay(w), lay(a), scan_p, s0l)
        y_nat = y_scan[:, :, :bh].reshape(seq, hd, bsz, heads_r).transpose(2, 0, 3, 1).reshape(bsz * seq, c)
        yrs.append(_mul_cast(y_nat, g))
        last = jnp.stack([lax.dynamic_slice(proj, (row0 + (b + 1) * seq - 1, ws), (1, proj_w - ws))[0]
                          for b in range(bsz)])
        outs0[name] = (hre.reshape(bsz, s5_g, s5_p).astype(state_s5_re.dtype),
                       him.reshape(bsz, s5_g, s5_p).astype(state_s5_im.dtype),
                       s_last[:, :, :bh].reshape(hd, hd, bsz, heads_r).transpose(2, 3, 1, 0).astype(state_rwkv.dtype),
                       last.astype(state_rwkv_shift.dtype))
    z = jnp.concatenate(zs, axis=0)
    zbf = jnp.concatenate(zbfs, axis=0)
    y_a = _mm(zbf, s5_w_glu.astype(BF16), out_dtype=BF16, name="mm_glu",
              epilogue=lambda acc, zt, b: zt * jax.nn.sigmoid(acc + b),
              extras=((z, 'tile'), (s5_b_glu.astype(F32).reshape(1, ws), 'row')))
    mixed = jnp.concatenate([y_a, jnp.concatenate(yrs, axis=0)], axis=1)
    mix = _mm(mixed, l0_w_out.astype(BF16), name="mm_l0_out")
    x, x_bf = _ln_res(x, [mix], ln_mix_g[0].astype(F32), ln_mix_b[0].astype(F32), alpha)
    routed, shared = _moe_ffn(x, x_bf, 0, *moe_p)
    x, x_bf = _ln_res(x, [routed, shared], ln_ffn_g[0].astype(F32), ln_ffn_b[0].astype(F32), alpha)

    ql, heads_a, qd = mla_w_uq.shape
    kl, _, nope = mla_w_uk.shape
    vd = mla_w_uv.shape[2]
    rope = qd - nope
    half = rope // 2
    past = cache_c_kv.shape[1]
    assert nope == LANES and vd == LANES and rope <= LANES
    scale = float(qd) ** -0.5
    w1 = jnp.pad(l1_w_in, ((0, 0), (0, LANES - rope))).astype(BF16)
    proj1 = _mm(x_bf, w1, tm=512, tn=ql + kl + LANES, name="mm_l1_in")
    pos = jnp.concatenate([jnp.tile(jnp.arange(lp_, dtype=jnp.int32), bp),
                           jnp.tile(past + jnp.arange(ls, dtype=jnp.int32), bs)])
    cos, sin = _rope_tables(pos, half)
    cqn, cqn_t, ckv, ckr, ckv_t, kr = _mla_prep(proj1, mla_q_norm.astype(F32), mla_kv_norm.astype(F32),
                                                cos, sin, ql, kl, half)
    w_qn3 = mla_w_uq[:, :, :nope]
    w_qr3 = jnp.pad(mla_w_uq[:, :, nope:], ((0, 0), (0, 0), (0, LANES - rope)))
    w_qt = jnp.stack([w_qn3, w_qr3], axis=2).reshape(ql, heads_a * 2 * LANES).T.astype(BF16)

    def rope_rows_epilogue(acc, cos_t, sin_t):
        out = []
        for j in range(acc.shape[0] // LANES):
            blk = acc[j * LANES:(j + 1) * LANES, :]
            if j % 2 == 1:
                row = lax.broadcasted_iota(jnp.int32, blk.shape, 0)
                swapped = jnp.where(row < half, pltpu.roll(blk, LANES - half, 0), pltpu.roll(blk, half, 0))
                blk = blk * cos_t + swapped * sin_t
            out.append(blk)
        return jnp.concatenate(out, axis=0)

    q_t = _mm(w_qt, cqn_t, n=n_p, out_dtype=BF16, epilogue=rope_rows_epilogue, name="mm_qt",
              extras=((cos[:n_p].T, 'coltile'), (sin[:n_p].T, 'coltile')))
    eye_rows = jnp.broadcast_to(jnp.eye(LANES, dtype=F32)[:, None, :], (LANES, heads_a, LANES))
    w_k = jnp.concatenate([jnp.stack([mla_w_uk.astype(F32), jnp.zeros((kl, heads_a, LANES), F32)], axis=2),
                           jnp.stack([jnp.zeros((LANES, heads_a, nope), F32), eye_rows], axis=2)], axis=0)
    k_cat = _mm(ckr, w_k.reshape(kl + LANES, heads_a * 2 * LANES).astype(BF16), m=n_p, out_dtype=BF16, name="mm_kcat")
    v_t = _mm(mla_w_uv.reshape(kl, heads_a * vd).T.astype(BF16), ckv_t, n=n_p, out_dtype=BF16, name="mm_vt")
    o_p = _attn_prompt(q_t, k_cat, v_t, bp, lp_, heads_a, scale)
    cqn_s = cqn[n_p:]

    def rope_epilogue(acc, cos_t, sin_t):
        return jnp.concatenate([_rope_lanes(acc[:, j * LANES:(j + 1) * LANES], cos_t, sin_t, half)
                                for j in range(acc.shape[1] // LANES)], axis=1)

    q_nope_s = _mm(cqn_s, w_qn3.reshape(ql, heads_a * nope).astype(BF16), out_dtype=BF16)
    q_rope_s = _mm(cqn_s, w_qr3.reshape(ql, heads_a * LANES).astype(BF16), out_dtype=BF16, epilogue=rope_epilogue,
                   extras=((cos[n_p:], 'rowtile'), (sin[n_p:], 'rowtile')))
    q_lat = _bmm_heads(q_nope_s, mla_w_uk.transpose(1, 2, 0).astype(BF16))
    n_keys = past + ls
    nkp = -(-n_keys // LATENT_T) * LATENT_T
    kv_all = jnp.concatenate([cache_c_kv.astype(F32), ckv[n_p:].reshape(bs, ls, kl)], axis=1)
    kr_all = jnp.concatenate([jnp.pad(cache_k_rope.astype(F32), ((0, 0), (0, 0), (0, LANES - rope))),
                              kr[n_p:].reshape(bs, ls, LANES)], axis=1)
    padk = lambda t: jnp.pad(t, ((0, 0), (0, nkp - n_keys), (0, 0))).astype(BF16).reshape(bs * nkp, -1)
    o_lat = _attn_latent(q_lat.reshape(n_s * heads_a, kl), q_rope_s.reshape(n_s * heads_a, LANES),
                         padk(kv_all), padk(kr_all), bs, ls * heads_a, heads_a, past, n_keys, scale)
    o_s = _bmm_heads(o_lat.reshape(n_s, heads_a * kl), mla_w_uv.transpose(1, 0, 2).astype(BF16))
    mix = _mm(jnp.concatenate([o_p, o_s], axis=0), l1_w_out.astype(BF16), name="mm_l1_out")
    x, x_bf = _ln_res(x, [mix], ln_mix_g[1].astype(F32), ln_mix_b[1].astype(F32), alpha)
    routed, shared = _moe_ffn(x, x_bf, 1, *moe_p)
    x, _ = _ln_res(x, [routed, shared], ln_ffn_g[1].astype(F32), ln_ffn_b[1].astype(F32), alpha)

    dt = x_prompt.dtype
    y_prompt = x[:n_p].reshape(bp, lp_, d).astype(dt)
    y_sample = x[n_p:].reshape(bs, ls, d).astype(dt)
    ckv_p = ckv[:n_p].reshape(bp, lp_, kl).astype(dt)
    ckv_s = ckv[n_p:].reshape(bs, ls, kl).astype(dt)
    kr_p = kr[:n_p, :rope].reshape(bp, lp_, rope).astype(dt)
    kr_s = kr[n_p:, :rope].reshape(bs, ls, rope).astype(dt)
    return (y_prompt, y_sample, *outs0["p"], ckv_p, kr_p, *outs0["s"], ckv_s, kr_s)
```
